```python
import math
import jax, jax.numpy as jnp
from jax import lax
import numpy as np

D_MODEL = 2048
BATCH = 4
SEQ = 4096
DEPTH = 2

HEAD_DIM = 128
A_WIDTH = D_MODEL // 2
A_GROUPS = A_WIDTH // 128
CHUNK = 128
B_HEADS = (D_MODEL // 2) // HEAD_DIM
B_WIDTH = B_HEADS * HEAD_DIM
DILATED_PAIRS = ((128, 1), (512, 4), (2048, 16))
ATT_BLOCK = 128
C_HEADS = D_MODEL // HEAD_DIM
C_WIDTH = C_HEADS * HEAD_DIM
D_FF = 4 * D_MODEL
N_EVEN = (DEPTH + 1) // 2
N_ODD = DEPTH // 2
RMS_EPS = 1e-6
LN_EPS = 1e-5

kernel_name = "hybrid_gmlp_dilated_stickbreak_trunk"


def rmsnorm(x, g):
    xf = x.astype(jnp.float32)
    y = xf * lax.rsqrt(jnp.mean(xf * xf, axis=-1, keepdims=True) + RMS_EPS)
    return (y * g.astype(jnp.float32)).astype(x.dtype)


def layernorm(x, g, b):
    xf = x.astype(jnp.float32)
    mu = jnp.mean(xf, axis=-1, keepdims=True)
    var = jnp.mean(jnp.square(xf - mu), axis=-1, keepdims=True)
    y = (xf - mu) * lax.rsqrt(var + LN_EPS)
    return (y * g.astype(jnp.float32) + b.astype(jnp.float32)).astype(x.dtype)


def alibi_slopes(n):
    return jnp.exp2(-8.0 * (jnp.arange(n, dtype=jnp.float32) + 1.0) / n)


def spatial_gating_unit(u, v, ln_g, ln_b, w_s, b_s):
    bsz, t, _ = v.shape
    v = layernorm(v, ln_g, ln_b)
    vg = v.reshape(bsz, t // CHUNK, CHUNK, A_GROUPS, A_WIDTH // A_GROUPS)
    w = jnp.tril(w_s).astype(v.dtype)
    mixed = jnp.einsum('gij,bcjgd->bcigd', w, vg) + b_s.T.astype(v.dtype)[None, None, :, :, None]
    return u * mixed.reshape(bsz, t, A_WIDTH)


def dilated_branch(q, k, v, window, dilation, slopes):
    bsz, t, h, dh = q.shape
    blk = ATT_BLOCK
    win_sub = window // dilation
    period = dilation * blk
    tp = -(-t // period) * period
    pad = tp - t
    sub_len = tp // dilation
    nb = sub_len // blk

    def to_sub(a):
        a = jnp.pad(a, ((0, 0), (0, pad), (0, 0), (0, 0)))
        a = a.reshape(bsz, sub_len, dilation, h, dh).transpose(0, 2, 3, 1, 4)
        return a.reshape(bsz, dilation, h, nb, blk, dh)

    qs, ks, vs = to_sub(q), to_sub(k), to_sub(v)

    def with_prev(a):
        prev = jnp.concatenate([jnp.zeros_like(a[:, :, :, :1]), a[:, :, :, :-1]], axis=3)
        return jnp.concatenate([prev, a], axis=4)

    kw, vw = with_prev(ks), with_prev(vs)
    s = jnp.einsum('brhnqe,brhnke->brhnqk', qs, kw).astype(jnp.float32)

    qi = jnp.arange(blk)[:, None]
    kj = jnp.arange(2 * blk)[None, :]
    dist = qi + blk - kj
    band = (dist >= 0) & (dist <= win_sub)
    first = (jnp.arange(nb) == 0)[:, None, None] & (kj < blk)[None]
    valid = band[None] & ~first
    bias = -slopes[:, None, None] * (dist * dilation).astype(jnp.float32)[None]
    s = s + bias[None, None, :, None]
    s = jnp.where(valid[None, None, None], s, -jnp.inf)
    m = jnp.max(s, axis=-1, keepdims=True)
    p = jnp.exp(s - m)
    den = jnp.sum(p, axis=-1, keepdims=True)
    o = jnp.einsum('brhnqk,brhnke->brhnqe', p, vw.astype(jnp.float32)) / den
    lse = (m + jnp.log(den))[..., 0]

    o = o.reshape(bsz, dilation, h, sub_len, dh).transpose(0, 3, 1, 2, 4).reshape(bsz, tp, h, dh)[:, :t]
    lse = lse.reshape(bsz, dilation, h, sub_len).transpose(0, 3, 1, 2).reshape(bsz, tp, h)[:, :t]
    return o, lse


def dilated_mixture(q, k, v):
    slopes = alibi_slopes(q.shape[2])
    outs, lses = [], []
    for window, dilation in DILATED_PAIRS:
        o, lse = dilated_branch(q, k, v, window, dilation, slopes)
        outs.append(o)
        lses.append(lse)
    wts = jax.nn.softmax(jnp.stack(lses, axis=0), axis=0)
    o = jnp.sum(wts[..., None] * jnp.stack(outs, axis=0), axis=0)
    return o.astype(q.dtype)


def gmlp_dilated_layer(h, w_in, ln_g, ln_b, w_s, b_s, w_out):
    bsz, t, _ = h.shape
    z = h @ w_in
    u, va, q, k, vb = jnp.split(z, [A_WIDTH, 2 * A_WIDTH, 2 * A_WIDTH + B_WIDTH, 2 * A_WIDTH + 2 * B_WIDTH], axis=-1)
    a_out = spatial_gating_unit(jax.nn.gelu(u, approximate=False), jax.nn.gelu(va, approximate=False), ln_g, ln_b, w_s, b_s)
    q = q.reshape(bsz, t, B_HEADS, HEAD_DIM) * (HEAD_DIM ** -0.5)
    k = k.reshape(bsz, t, B_HEADS, HEAD_DIM)
    vb = vb.reshape(bsz, t, B_HEADS, HEAD_DIM)
    b_out = dilated_mixture(q, k, vb).reshape(bsz, t, B_WIDTH)
    return jnp.concatenate([a_out, b_out], axis=-1) @ w_out


def stick_breaking_attention(q, k, v):
    bsz, h, t, dh = q.shape
    nb = t // ATT_BLOCK
    qb = q.reshape(bsz, h, nb, ATT_BLOCK, dh).transpose(2, 0, 1, 3, 4)
    kpos = jnp.arange(t)

    def block(args):
        qblk, start = args
        z = jnp.einsum('bhqe,bhke->bhqk', qblk, k).astype(jnp.float32)
        qpos = start + jnp.arange(ATT_BLOCK)
        causal = kpos[None, :] < qpos[:, None]
        log_1m_beta = jnp.where(causal, jax.nn.log_sigmoid(-z), 0.0)
        rev = lax.cumsum(log_1m_beta, axis=3, reverse=True)
        excl = jnp.concatenate([rev[..., 1:], jnp.zeros_like(rev[..., :1])], axis=-1)
        a = jnp.where(causal, jnp.exp(jax.nn.log_sigmoid(z) + excl), 0.0)
        return jnp.einsum('bhqk,bhke->bhqe', a.astype(v.dtype), v)

    o = lax.map(block, (qb, jnp.arange(nb) * ATT_BLOCK))
    return o.transpose(1, 0, 3, 2, 4).reshape(bsz, t, h * dh)


def stick_breaking_layer(h, w_in, w_out):
    bsz, t, _ = h.shape
    q, k, v = jnp.split(h @ w_in, 3, axis=-1)
    def heads(a):
        return a.reshape(bsz, t, C_HEADS, HEAD_DIM).transpose(0, 2, 1, 3)
    o = stick_breaking_attention(heads(q) * (HEAD_DIM ** -0.5), heads(k), heads(v))
    return o @ w_out


def squared_relu_mlp(h, w1, w2):
    return jnp.square(jax.nn.relu(h @ w1)) @ w2


def setup_inputs(seed: int = 0) -> dict:
    key = jax.random.key(seed)
    ks = jax.random.split(key, 16)
    f32 = jnp.float32
    def nrm(k, shape, scale):
        return jax.random.normal(k, shape, f32) * scale
    def gain(k, shape):
        return 1.0 + 0.02 * jax.random.normal(k, shape, f32)
    return {
        "x": jax.random.normal(ks[0], (BATCH, SEQ, D_MODEL), f32),
        "norm_pre_mix": gain(ks[1], (DEPTH, D_MODEL)),
        "norm_post_mix": gain(ks[2], (DEPTH, D_MODEL)),
        "norm_pre_ffn": gain(ks[3], (DEPTH, D_MODEL)),
        "norm_post_ffn": gain(ks[4], (DEPTH, D_MODEL)),
        "ab_w_in": nrm(ks[5], (N_EVEN, D_MODEL, 2 * A_WIDTH + 3 * B_WIDTH), D_MODEL ** -0.5),
        "sgu_ln_g": gain(ks[6], (N_EVEN, A_WIDTH)),
        "sgu_ln_b": nrm(ks[7], (N_EVEN, A_WIDTH), 0.02),
        "sgu_w": nrm(ks[8], (N_EVEN, A_GROUPS, CHUNK, CHUNK), CHUNK ** -0.5),
        "sgu_b": 1.0 + nrm(ks[9], (N_EVEN, A_GROUPS, CHUNK), 0.1),
        "ab_w_out": nrm(ks[10], (N_EVEN, A_WIDTH + B_WIDTH, D_MODEL), (A_WIDTH + B_WIDTH) ** -0.5),
        "sb_w_in": nrm(ks[11], (N_ODD, D_MODEL, 3 * C_WIDTH), D_MODEL ** -0.5),
        "sb_w_out": nrm(ks[12], (N_ODD, C_WIDTH, D_MODEL), C_WIDTH ** -0.5),
        "ffn_w1": nrm(ks[13], (DEPTH, D_MODEL, D_FF), D_MODEL ** -0.5),
        "ffn_w2": nrm(ks[14], (DEPTH, D_FF, D_MODEL), D_FF ** -0.5),
    }


def reference(x, norm_pre_mix, norm_post_mix, norm_pre_ffn, norm_post_ffn,
              ab_w_in, sgu_ln_g, sgu_ln_b, sgu_w, sgu_b, ab_w_out,
              sb_w_in, sb_w_out, ffn_w1, ffn_w2):
    for layer in range(DEPTH):
        h = rmsnorm(x, norm_pre_mix[layer])
        i = layer // 2
        if layer % 2 == 0:
            y = gmlp_dilated_layer(h, ab_w_in[i], sgu_ln_g[i], sgu_ln_b[i], sgu_w[i], sgu_b[i], ab_w_out[i])
        else:
            y = stick_breaking_layer(h, sb_w_in[i], sb_w_out[i])
        x = x + rmsnorm(y, norm_post_mix[layer])
        h = rmsnorm(x, norm_pre_ffn[layer])
        y = squared_relu_mlp(h, ffn_w1[layer], ffn_w2[layer])
        x = x + rmsnorm(y, norm_post_ffn[layer])
    return x
```

```python
import functools
import math

import jax
import jax.numpy as jnp
from jax import lax
from jax.experimental import pallas as pl
from jax.experimental.pallas import tpu as pltpu

HEAD_DIM = 128
CHUNK = 128
ATT_BLOCK = 128
DILATED_PAIRS = ((128, 1), (512, 4), (2048, 16))
RMS_EPS = 1e-6
LN_EPS = 1e-5
NEG_BIG = -1e30

V7X_VMEM_LIMIT_BYTES = 56 * 1024 * 1024

BF16 = jnp.bfloat16
F32 = jnp.float32


def _params(*semantics):
    return pltpu.CompilerParams(dimension_semantics=semantics, vmem_limit_bytes=V7X_VMEM_LIMIT_BYTES)


def _rms_scale(x):
    return lax.rsqrt(jnp.mean(x * x, axis=-1, keepdims=True) + RMS_EPS)


def _gelu(x):
    return 0.5 * x * (1.0 + lax.erf(x * math.sqrt(0.5)))


def _norm_matmul_kernel(x_ref, g_ref, w_ref, lng_ref, lnb_ref, o_ref, h_ref, *, sgu_epilogue):
    j = pl.program_id(1)

    @pl.when(j == 0)
    def _():
        x = x_ref[...]
        h_ref[...] = (x * _rms_scale(x) * g_ref[...]).astype(h_ref.dtype)

    acc = jnp.dot(h_ref[...], w_ref[...], preferred_element_type=F32)
    if not sgu_epilogue:
        o_ref[...] = acc.astype(o_ref.dtype)
        return

    @pl.when(j == 0)
    def _():
        o_ref[...] = _gelu(acc).astype(o_ref.dtype)

    @pl.when(j == 1)
    def _():
        v = _gelu(acc)
        mu = jnp.mean(v, axis=-1, keepdims=True)
        c = v - mu
        var = jnp.mean(c * c, axis=-1, keepdims=True)
        o_ref[...] = (c * lax.rsqrt(var + LN_EPS) * lng_ref[...] + lnb_ref[...]).astype(o_ref.dtype)

    @pl.when(j >= 2)
    def _():
        o_ref[...] = acc.astype(o_ref.dtype)


def _norm_matmul(x, g, w, ln_g, ln_b, *, sgu_epilogue, tm=512, tn=1024):
    m, d = x.shape
    n = w.shape[1]
    return pl.pallas_call(
        functools.partial(_norm_matmul_kernel, sgu_epilogue=sgu_epilogue),
        out_shape=jax.ShapeDtypeStruct((m, n), BF16),
        grid=(m // tm, n // tn),
        in_specs=[
            pl.BlockSpec((tm, d), lambda i, j: (i, 0)),
            pl.BlockSpec((1, d), lambda i, j: (0, 0)),
            pl.BlockSpec((d, tn), lambda i, j: (0, j)),
            pl.BlockSpec((1, tn), lambda i, j: (0, 0)),
            pl.BlockSpec((1, tn), lambda i, j: (0, 0)),
        ],
        out_specs=pl.BlockSpec((tm, tn), lambda i, j: (i, j)),
        scratch_shapes=[pltpu.VMEM((tm, d), BF16)],
        compiler_params=_params("parallel", "arbitrary"),
        name="norm_matmul_sgu" if sgu_epilogue else "norm_matmul",
    )(x, g, w, ln_g, ln_b)


def _sgu_kernel(u_ref, v_ref, w_ref, b_ref, o_ref, *, chunks, groups):
    row = lax.broadcasted_iota(jnp.int32, (CHUNK, CHUNK), 0)
    col = lax.broadcasted_iota(jnp.int32, (CHUNK, CHUNK), 1)
    for g in range(groups):
        cs = slice(g * CHUNK, (g + 1) * CHUNK)
        wg = jnp.where(row >= col, w_ref[g], 0.0).astype(BF16)
        bias = b_ref[:, cs]
        for c in range(chunks):
            rs = slice(c * CHUNK, (c + 1) * CHUNK)
            mixed = jnp.dot(wg, v_ref[rs, cs], preferred_element_type=F32) + bias
            o_ref[rs, cs] = (u_ref[rs, cs].astype(F32) * mixed).astype(o_ref.dtype)


def _sgu(z, w_s, bias_full, *, a_width, chunks=4):
    m = z.shape[0]
    groups = w_s.shape[0]
    tm = chunks * CHUNK
    return pl.pallas_call(
        functools.partial(_sgu_kernel, chunks=chunks, groups=groups),
        out_shape=jax.ShapeDtypeStruct((m, a_width), BF16),
        grid=(m // tm,),
        in_specs=[
            pl.BlockSpec((tm, a_width), lambda i: (i, 0)),
            pl.BlockSpec((tm, a_width), lambda i: (i, 1)),
            pl.BlockSpec((groups, CHUNK, CHUNK), lambda i: (0, 0, 0)),
            pl.BlockSpec((CHUNK, a_width), lambda i: (0, 0)),
        ],
        out_specs=pl.BlockSpec((tm, a_width), lambda i: (i, 0)),
        compiler_params=_params("parallel"),
        name="sgu",
    )(z, z, w_s, bias_full)


def _dilated_kernel(q_ref, kp_ref, kc_ref, vp_ref, vc_ref, bias_ref, o_ref, lse_ref, *, n_heads, scale):
    first = jnp.where(pl.program_id(2) == 0, 0, 1)
    lane = lax.broadcasted_iota(jnp.int32, (ATT_BLOCK, 128), 1)
    lse = jnp.zeros((ATT_BLOCK, 128), F32)
    for h in range(n_heads):
        sl = slice(h * HEAD_DIM, (h + 1) * HEAD_DIM)
        k = jnp.concatenate([kp_ref[:, sl], kc_ref[:, sl]], axis=0)
        v = jnp.concatenate([vp_ref[:, sl], vc_ref[:, sl]], axis=0)
        s = lax.dot_general(q_ref[:, sl], k, (((1,), (1,)), ((), ())), preferred_element_type=F32)
        s = s * scale + bias_ref[first, h]
        m = jnp.max(s, axis=-1, keepdims=True)
        p = jnp.exp(s - m)
        den = jnp.sum(p, axis=-1, keepdims=True)
        o = jnp.dot(p.astype(BF16), v, preferred_element_type=F32) * (1.0 / den)
        o_ref[:, sl] = o.astype(o_ref.dtype)
        lse = jnp.where(lane == h, m + jnp.log(den), lse)
    lse_ref[...] = lse


def _dilated_bias(window, dilation, n_heads):
    blk = ATT_BLOCK
    win_sub = window // dilation
    qi = jnp.arange(blk)[:, None]
    kj = jnp.arange(2 * blk)[None, :]
    dist = qi + blk - kj
    band = (dist >= 0) & (dist <= win_sub)
    slopes = jnp.exp2(-8.0 * (jnp.arange(n_heads, dtype=F32) + 1.0) / n_heads)
    bias = -slopes[:, None, None] * (dist * dilation).astype(F32)[None]
    later = jnp.where(band[None], bias, NEG_BIG)
    first = jnp.where((band & (kj >= blk))[None], bias, NEG_BIG)
    return jnp.stack([first, later], axis=0)


def _dilated_branch(z, bsz, t, window, dilation, *, n_heads, q_col):
    row = z.shape[1]
    width = n_heads * HEAD_DIM
    per_row = row // width
    d = dilation
    sub_len = t // d
    nb = sub_len // ATT_BLOCK
    zr = z.reshape(bsz, sub_len, d * row)
    bias = _dilated_bias(window, d, n_heads)

    def cur(off):
        return pl.BlockSpec((None, ATT_BLOCK, width), lambda b, r, n: (b, n, r * per_row + off))

    def prev(off):
        return pl.BlockSpec((None, ATT_BLOCK, width), lambda b, r, n: (b, jnp.maximum(n - 1, 0), r * per_row + off))

    o, lse = pl.pallas_call(
        functools.partial(_dilated_kernel, n_heads=n_heads, scale=HEAD_DIM ** -0.5),
        out_shape=(jax.ShapeDtypeStruct((bsz, sub_len, d * width), BF16),
                   jax.ShapeDtypeStruct((bsz, sub_len, d * 128), F32)),
        grid=(bsz, d, nb),
        in_specs=[cur(q_col), prev(q_col + 1), cur(q_col + 1), prev(q_col + 2), cur(q_col + 2),
                  pl.BlockSpec((2, n_heads, ATT_BLOCK, 2 * ATT_BLOCK), lambda b, r, n: (0, 0, 0, 0))],
        out_specs=(pl.BlockSpec((None, ATT_BLOCK, width), lambda b, r, n: (b, n, r)),
                   pl.BlockSpec((None, ATT_BLOCK, 128), lambda b, r, n: (b, n, r))),
        compiler_params=_params("parallel", "parallel", "arbitrary"),
        name=f"dilated_d{d}",
    )(zr, zr, zr, zr, zr, bias)
    return o.reshape(bsz * t, width), lse.reshape(bsz * t, 128)


def _post_norm_residual(x_ref, y, g_ref, o_ref):
    o_ref[...] = x_ref[...] + y * _rms_scale(y) * g_ref[...]


def _out_merge_kernel(a_ref, o1_ref, o2_ref, o3_ref, l1_ref, l2_ref, l3_ref, w_ref, x_ref, g_ref, out_ref,
                      *, n_heads, a_width):
    l1, l2, l3 = l1_ref[...], l2_ref[...], l3_ref[...]
    m = jnp.maximum(jnp.maximum(l1, l2), l3)
    e1, e2, e3 = jnp.exp(l1 - m), jnp.exp(l2 - m), jnp.exp(l3 - m)
    inv = 1.0 / (e1 + e2 + e3)
    w1, w2, w3 = e1 * inv, e2 * inv, e3 * inv
    parts = []
    for h in range(n_heads):
        sl = slice(h * HEAD_DIM, (h + 1) * HEAD_DIM)
        bh = (w1[:, h:h + 1] * o1_ref[:, sl].astype(F32)
              + w2[:, h:h + 1] * o2_ref[:, sl].astype(F32)
              + w3[:, h:h + 1] * o3_ref[:, sl].astype(F32))
        parts.append(bh.astype(BF16))
    b_out = jnp.concatenate(parts, axis=1)
    y = (jnp.dot(a_ref[...], w_ref[:a_width, :], preferred_element_type=F32)
         + jnp.dot(b_out, w_ref[a_width:, :], preferred_element_type=F32))
    _post_norm_residual(x_ref, y, g_ref, out_ref)


def _out_merge(a_out, outs, lses, w_out, x, g, *, n_heads, tm=512):
    m, d = x.shape
    a_width = a_out.shape[1]
    b_width = outs[0].shape[1]
    row = lambda width: pl.BlockSpec((tm, width), lambda i: (i, 0))
    const = lambda shape: pl.BlockSpec(shape, lambda i: (0, 0))
    return pl.pallas_call(
        functools.partial(_out_merge_kernel, n_heads=n_heads, a_width=a_width),
        out_shape=jax.ShapeDtypeStruct((m, d), F32),
        grid=(m // tm,),
        in_specs=[row(a_width), row(b_width), row(b_width), row(b_width), row(128), row(128), row(128),
                  const(w_out.shape), row(d), const((1, d))],
        out_specs=row(d),
        compiler_params=_params("parallel"),
        name="out_merge",
    )(a_out, *outs, *lses, w_out, x, g)


def _out_proj_kernel(a_ref, w_ref, x_ref, g_ref, out_ref):
    y = jnp.dot(a_ref[...], w_ref[...], preferred_element_type=F32)
    _post_norm_residual(x_ref, y, g_ref, out_ref)


def _out_proj(a, w_out, x, g, *, tm=512):
    m, d = x.shape
    row = lambda width: pl.BlockSpec((tm, width), lambda i: (i, 0))
    const = lambda shape: pl.BlockSpec(shape, lambda i: (0, 0))
    return pl.pallas_call(
        _out_proj_kernel,
        out_shape=jax.ShapeDtypeStruct((m, d), F32),
        grid=(m // tm,),
        in_specs=[row(a.shape[1]), const(w_out.shape), row(d), const((1, d))],
        out_specs=row(d),
        compiler_params=_params("parallel"),
        name="out_proj",
    )(a, w_out, x, g)


def _ffn_kernel(x_ref, g1_ref, w1_ref, w2_ref, g2_ref, o_ref, h_ref, acc_ref):
    f = pl.program_id(1)

    @pl.when(f == 0)
    def _():
        x = x_ref[...]
        h_ref[...] = (x * _rms_scale(x) * g1_ref[...]).astype(h_ref.dtype)
        acc_ref[...] = jnp.zeros_like(acc_ref)

    t = jnp.maximum(jnp.dot(h_ref[...], w1_ref[...], preferred_element_type=F32), 0.0)
    acc_ref[...] += jnp.dot((t * t).astype(BF16), w2_ref[...], preferred_element_type=F32)

    @pl.when(f == pl.num_programs(1) - 1)
    def _():
        _post_norm_residual(x_ref, acc_ref[...], g2_ref, o_ref)


def _ffn(x, g1, w1, w2, g2, *, tm=512, tf=1024):
    m, d = x.shape
    d_ff = w1.shape[1]
    return pl.pallas_call(
        _ffn_kernel,
        out_shape=jax.ShapeDtypeStruct((m, d), F32),
        grid=(m // tm, d_ff // tf),
        in_specs=[
            pl.BlockSpec((tm, d), lambda i, f: (i, 0)),
            pl.BlockSpec((1, d), lambda i, f: (0, 0)),
            pl.BlockSpec((d, tf), lambda i, f: (0, f)),
            pl.BlockSpec((tf, d), lambda i, f: (f, 0)),
            pl.BlockSpec((1, d), lambda i, f: (0, 0)),
        ],
        out_specs=pl.BlockSpec((tm, d), lambda i, f: (i, 0)),
        scratch_shapes=[pltpu.VMEM((tm, d), BF16), pltpu.VMEM((tm, d), F32)],
        compiler_params=_params("parallel", "arbitrary"),
        name="ffn",
    )(x, g1, w1, w2, g2)


def _sb_kernel(q_ref, k_ref, v_ref, tri_ref, o_ref, *, blk, scale):
    i = pl.program_id(2)
    q = q_ref[...]
    tri = tri_ref[...]
    row = lax.broadcasted_iota(jnp.int32, (blk, blk), 0)
    col = lax.broadcasted_iota(jnp.int32, (blk, blk), 1)
    causal = col < row

    def block(j, acc, suffix, diagonal):
        start = pl.multiple_of(j * blk, blk)
        k = k_ref[pl.ds(start, blk), :]
        v = v_ref[pl.ds(start, blk), :]
        z = lax.dot_general(q, k, (((1,), (1,)), ((), ())), preferred_element_type=F32) * scale
        sp = jnp.maximum(z, 0.0) + jnp.log1p(jnp.exp(-jnp.abs(z)))
        if diagonal:
            sp = jnp.where(causal, sp, 0.0)
        hi = sp.astype(BF16)
        lo = (sp - hi.astype(F32)).astype(BF16)
        cum = (jnp.dot(hi, tri, preferred_element_type=F32)
               + jnp.dot(lo, tri, preferred_element_type=F32) + suffix)
        a = jnp.exp(z - cum)
        if diagonal:
            a = jnp.where(causal, a, 0.0)
        acc = acc + jnp.dot(a.astype(BF16), v, preferred_element_type=F32)
        return acc, cum[:, 0:1]

    acc, suffix = block(i, jnp.zeros((blk, HEAD_DIM), F32), jnp.zeros((blk, 1), F32), True)

    def body(step, carry):
        return block(i - 1 - step, carry[0], carry[1], False)

    acc, _ = lax.fori_loop(0, i, body, (acc, suffix))
    o_ref[...] = acc.astype(o_ref.dtype)


def _stick_breaking(qkv, bsz, t, *, n_heads, blk=256):
    nq = t // blk
    ar = jnp.arange(blk)
    tri = (ar[:, None] >= ar[None, :]).astype(BF16)
    return pl.pallas_call(
        functools.partial(_sb_kernel, blk=blk, scale=HEAD_DIM ** -0.5),
        out_shape=jax.ShapeDtypeStruct((bsz * t, n_heads * HEAD_DIM), BF16),
        grid=(bsz, n_heads, nq),
        in_specs=[
            pl.BlockSpec((blk, HEAD_DIM), lambda b, h, i: (b * nq + i, h)),
            pl.BlockSpec((t, HEAD_DIM), lambda b, h, i: (b, n_heads + h)),
            pl.BlockSpec((t, HEAD_DIM), lambda b, h, i: (b, 2 * n_heads + h)),
            pl.BlockSpec((blk, blk), lambda b, h, i: (0, 0)),
        ],
        out_specs=pl.BlockSpec((blk, HEAD_DIM), lambda b, h, i: (b * nq + i, h)),
        compiler_params=_params("parallel", "parallel", "arbitrary"),
        name="stick_breaking",
    )(qkv, qkv, qkv, tri)


def kernel(x, norm_pre_mix, norm_post_mix, norm_pre_ffn, norm_post_ffn, ab_w_in, sgu_ln_g, sgu_ln_b, sgu_w, sgu_b,
           ab_w_out, sb_w_in, sb_w_out, ffn_w1, ffn_w2):
    bsz, t, d = x.shape
    a_width = sgu_ln_g.shape[-1]
    b_heads = (ab_w_in.shape[-1] - 2 * a_width) // (3 * HEAD_DIM)
    c_heads = sb_w_in.shape[-1] // (3 * HEAD_DIM)
    assert a_width == b_heads * HEAD_DIM, "dilated q/k/v column blocks are addressed in units of a_width"

    xs = x.reshape(bsz * t, d)
    row = lambda v: v.reshape(1, -1).astype(F32)

    z = _norm_matmul(xs, row(norm_pre_mix[0]), ab_w_in[0].astype(BF16), row(sgu_ln_g[0]), row(sgu_ln_b[0]),
                     sgu_epilogue=True, tn=a_width)
    bias_full = jnp.repeat(sgu_b[0].T.astype(F32), CHUNK, axis=1)
    a_out = _sgu(z, sgu_w[0], bias_full, a_width=a_width)
    outs, lses = zip(*[_dilated_branch(z, bsz, t, window, dilation, n_heads=b_heads, q_col=2)
                       for window, dilation in DILATED_PAIRS])
    xs = _out_merge(a_out, outs, lses, ab_w_out[0].astype(BF16), xs, row(norm_post_mix[0]), n_heads=b_heads)
    xs = _ffn(xs, row(norm_pre_ffn[0]), ffn_w1[0].astype(BF16), ffn_w2[0].astype(BF16), row(norm_post_ffn[0]))

    zeros = jnp.zeros((1, 1024), F32)
    qkv = _norm_matmul(xs, row(norm_pre_mix[1]), sb_w_in[0].astype(BF16), zeros, zeros, sgu_epilogue=False)
    att = _stick_breaking(qkv, bsz, t, n_heads=c_heads)
    xs = _out_proj(att, sb_w_out[0].astype(BF16), xs, row(norm_post_mix[1]))
    xs = _ffn(xs, row(norm_pre_ffn[1]), ffn_w1[1].astype(BF16), ffn_w2[1].astype(BF16), row(norm_post_ffn[1]))
    return xs.reshape(bsz, t, d)
```

```python
import functools
import math

import jax
import jax.numpy as jnp
from jax import lax
from jax.experimental import pallas as pl
from jax.experimental.pallas import tpu as pltpu

HEAD_DIM = 128
CHUNK = 128
ATT_BLOCK = 128
DILATED_PAIRS = ((128, 1), (512, 4), (2048, 16))
RMS_EPS = 1e-6
LN_EPS = 1e-5
NEG_BIG = -1e30

V7X_VMEM_LIMIT_BYTES = 56 * 1024 * 1024

BF16 = jnp.bfloat16
F32 = jnp.float32


def _params(*semantics):
    return pltpu.CompilerParams(dimension_semantics=semantics, vmem_limit_bytes=V7X_VMEM_LIMIT_BYTES)


def _rms_scale(x):
    return lax.rsqrt(jnp.mean(x * x, axis=-1, keepdims=True) + RMS_EPS)


def _gelu(x):
    return 0.5 * x * (1.0 + lax.erf(x * math.sqrt(0.5)))


def _norm_matmul_kernel(x_ref, g_ref, w_ref, lng_ref, lnb_ref, o_ref, h_ref, *, sgu_epilogue):
    j = pl.program_id(1)

    @pl.when(j == 0)
    def _():
        x = x_ref[...]
        h_ref[...] = (x * _rms_scale(x) * g_ref[...]).astype(h_ref.dtype)

    acc = jnp.dot(h_ref[...], w_ref[...], preferred_element_type=F32)
    if not sgu_epilogue:
        o_ref[...] = acc.astype(o_ref.dtype)
        return

    @pl.when(j == 0)
    def _():
        o_ref[...] = _gelu(acc).astype(o_ref.dtype)

    @pl.when(j == 1)
    def _():
        v = _gelu(acc)
        mu = jnp.mean(v, axis=-1, keepdims=True)
        c = v - mu
        var = jnp.mean(c * c, axis=-1, keepdims=True)
        o_ref[...] = (c * lax.rsqrt(var + LN_EPS) * lng_ref[...] + lnb_ref[...]).astype(o_ref.dtype)

    @pl.when(j >= 2)
    def _():
        o_ref[...] = acc.astype(o_ref.dtype)


def _norm_matmul(x, g, w, ln_g, ln_b, *, sgu_epilogue, tm=512, tn=1024):
    m, d = x.shape
    n = w.shape[1]
    return pl.pallas_call(
        functools.partial(_norm_matmul_kernel, sgu_epilogue=sgu_epilogue),
        out_shape=jax.ShapeDtypeStruct((m, n), BF16),
        grid=(m // tm, n // tn),
        in_specs=[
            pl.BlockSpec((tm, d), lambda i, j: (i, 0)),
            pl.BlockSpec((1, d), lambda i, j: (0, 0)),
            pl.BlockSpec((d, tn), lambda i, j: (0, j)),
            pl.BlockSpec((1, tn), lambda i, j: (0, 0)),
            pl.BlockSpec((1, tn), lambda i, j: (0, 0)),
        ],
        out_specs=pl.BlockSpec((tm, tn), lambda i, j: (i, j)),
        scratch_shapes=[pltpu.VMEM((tm, d), BF16)],
        compiler_params=_params("parallel", "arbitrary"),
        name="norm_matmul_sgu" if sgu_epilogue else "norm_matmul",
    )(x, g, w, ln_g, ln_b)


def _sgu_kernel(u_ref, v_ref, w_ref, b_ref, o_ref, *, chunks, groups):
    row = lax.broadcasted_iota(jnp.int32, (CHUNK, CHUNK), 0)
    col = lax.broadcasted_iota(jnp.int32, (CHUNK, CHUNK), 1)
    for g in range(groups):
        cs = slice(g * CHUNK, (g + 1) * CHUNK)
        wg = jnp.where(row >= col, w_ref[g], 0.0).astype(BF16)
        bias = b_ref[:, cs]
        for c in range(chunks):
            rs = slice(c * CHUNK, (c + 1) * CHUNK)
            mixed = jnp.dot(wg, v_ref[rs, cs], preferred_element_type=F32) + bias
            o_ref[rs, cs] = (u_ref[rs, cs].astype(F32) * mixed).astype(o_ref.dtype)


def _sgu(z, w_s, bias_full, *, a_width, chunks=4):
    m = z.shape[0]
    groups = w_s.shape[0]
    tm = chunks * CHUNK
    return pl.pallas_call(
        functools.partial(_sgu_kernel, chunks=chunks, groups=groups),
        out_shape=jax.ShapeDtypeStruct((m, a_width), BF16),
        grid=(m // tm,),
        in_specs=[
            pl.BlockSpec((tm, a_width), lambda i: (i, 0)),
            pl.BlockSpec((tm, a_width), lambda i: (i, 1)),
            pl.BlockSpec((groups, CHUNK, CHUNK), lambda i: (0, 0, 0)),
            pl.BlockSpec((CHUNK, a_width), lambda i: (0, 0)),
        ],
        out_specs=pl.BlockSpec((tm, a_width), lambda i: (i, 0)),
        compiler_params=_params("parallel"),
        name="sgu",
    )(z, z, w_s, bias_full)


def _dilated_kernel(q_ref, kp_ref, kc_ref, vp_ref, vc_ref, bias_ref, o_ref, lse_ref, *, n_heads, scale):
    first = jnp.where(pl.program_id(2) == 0, 0, 1)
    lane = lax.broadcasted_iota(jnp.int32, (ATT_BLOCK, 128), 1)
    lse = jnp.zeros((ATT_BLOCK, 128), F32)
    for h in range(n_heads):
        sl = slice(h * HEAD_DIM, (h + 1) * HEAD_DIM)
        k = jnp.concatenate([kp_ref[:, sl], kc_ref[:, sl]], axis=0)
        v = jnp.concatenate([vp_ref[:, sl], vc_ref[:, sl]], axis=0)
        s = lax.dot_general(q_ref[:, sl], k, (((1,), (1,)), ((), ())), preferred_element_type=F32)
        s = s * scale + bias_ref[first, h]
        m = jnp.max(s, axis=-1, keepdims=True)
        p = jnp.exp(s - m)
        den = jnp.sum(p, axis=-1, keepdims=True)
        o = jnp.dot(p.astype(BF16), v, preferred_element_type=F32) * (1.0 / den)
        o_ref[:, sl] = o.astype(o_ref.dtype)
        lse = jnp.where(lane == h, m + jnp.log(den), lse)
    lse_ref[...] = lse


def _dilated_bias(window, dilation, n_heads):
    blk = ATT_BLOCK
    win_sub = window // dilation
    qi = jnp.arange(blk)[:, None]
    kj = jnp.arange(2 * blk)[None, :]
    dist = qi + blk - kj
    band = (dist >= 0) & (dist <= win_sub)
    slopes = jnp.exp2(-8.0 * (jnp.arange(n_heads, dtype=F32) + 1.0) / n_heads)
    bias = -slopes[:, None, None] * (dist * dilation).astype(F32)[None]
    later = jnp.where(band[None], bias, NEG_BIG)
    first = jnp.where((band & (kj >= blk))[None], bias, NEG_BIG)
    return jnp.stack([first, later], axis=0)


def _dilated_branch(z, bsz, t, window, dilation, *, n_heads, q_col):
    row = z.shape[1]
    width = n_heads * HEAD_DIM
    per_row = row // width
    d = dilation
    sub_len = t // d
    nb = sub_len // ATT_BLOCK
    zr = z.reshape(bsz, sub_len, d * row)
    bias = _dilated_bias(window, d, n_heads)

    def cur(off):
        return pl.BlockSpec((None, ATT_BLOCK, width), lambda b, r, n: (b, n, r * per_row + off))

    def prev(off):
        return pl.BlockSpec((None, ATT_BLOCK, width), lambda b, r, n: (b, jnp.maximum(n - 1, 0), r * per_row + off))

    o, lse = pl.pallas_call(
        functools.partial(_dilated_kernel, n_heads=n_heads, scale=HEAD_DIM ** -0.5),
        out_shape=(jax.ShapeDtypeStruct((bsz, sub_len, d * width), BF16),
                   jax.ShapeDtypeStruct((bsz, sub_len, d * 128), F32)),
        grid=(bsz, d, nb),
        in_specs=[cur(q_col), prev(q_col + 1), cur(q_col + 1), prev(q_col + 2), cur(q_col + 2),
                  pl.BlockSpec((2, n_heads, ATT_BLOCK, 2 * ATT_BLOCK), lambda b, r, n: (0, 0, 0, 0))],
        out_specs=(pl.BlockSpec((None, ATT_BLOCK, width), lambda b, r, n: (b, n, r)),
                   pl.BlockSpec((None, ATT_BLOCK, 128), lambda b, r, n: (b, n, r))),
        compiler_params=_params("parallel", "parallel", "arbitrary"),
        name=f"dilated_d{d}",
    )(zr, zr, zr, zr, zr, bias)
    return o.reshape(bsz * t, width), lse.reshape(bsz * t, 128)


def _post_norm_residual(x_ref, y, g_ref, o_ref):
    o_ref[...] = x_ref[...] + y * _rms_scale(y) * g_ref[...]


def _out_merge_kernel(a_ref, o1_ref, o2_ref, o3_ref, l1_ref, l2_ref, l3_ref, w_ref, x_ref, g_ref, out_ref,
                      *, n_heads, a_width):
    l1, l2, l3 = l1_ref[...], l2_ref[...], l3_ref[...]
    m = jnp.maximum(jnp.maximum(l1, l2), l3)
    e1, e2, e3 = jnp.exp(l1 - m), jnp.exp(l2 - m), jnp.exp(l3 - m)
    inv = 1.0 / (e1 + e2 + e3)
    w1, w2, w3 = e1 * inv, e2 * inv, e3 * inv
    parts = []
    for h in range(n_heads):
        sl = slice(h * HEAD_DIM, (h + 1) * HEAD_DIM)
        bh = (w1[:, h:h + 1] * o1_ref[:, sl].astype(F32)
              + w2[:, h:h + 1] * o2_ref[:, sl].astype(F32)
              + w3[:, h:h + 1] * o3_ref[:, sl].astype(F32))
        parts.append(bh.astype(BF16))
    b_out = jnp.concatenate(parts, axis=1)
    y = (jnp.dot(a_ref[...], w_ref[:a_width, :], preferred_element_type=F32)
         + jnp.dot(b_out, w_ref[a_width:, :], preferred_element_type=F32))
    _post_norm_residual(x_ref, y, g_ref, out_ref)


def _out_merge(a_out, outs, lses, w_out, x, g, *, n_heads, tm=512):
    m, d = x.shape
    a_width = a_out.shape[1]
    b_width = outs[0].shape[1]
    row = lambda width: pl.BlockSpec((tm, width), lambda i: (i, 0))
    const = lambda shape: pl.BlockSpec(shape, lambda i: (0, 0))
    return pl.pallas_call(
        functools.partial(_out_merge_kernel, n_heads=n_heads, a_width=a_width),
        out_shape=jax.ShapeDtypeStruct((m, d), F32),
        grid=(m // tm,),
        in_specs=[row(a_width), row(b_width), row(b_width), row(b_width), row(128), row(128), row(128),
                  const(w_out.shape), row(d), const((1, d))],
        out_specs=row(d),
        compiler_params=_params("parallel"),
        name="out_merge",
    )(a_out, *outs, *lses, w_out, x, g)


def _out_proj_kernel(a_ref, w_ref, x_ref, g_ref, out_ref):
    y = jnp.dot(a_ref[...], w_ref[...], preferred_element_type=F32)
    _post_norm_residual(x_ref, y, g_ref, out_ref)


def _out_proj(a, w_out, x, g, *, tm=512):
    m, d = x.shape
    row = lambda width: pl.BlockSpec((tm, width), lambda i: (i, 0))
    const = lambda shape: pl.BlockSpec(shape, lambda i: (0, 0))
    return pl.pallas_call(
        _out_proj_kernel,
        out_shape=jax.ShapeDtypeStruct((m, d), F32),
        grid=(m // tm,),
        in_specs=[row(a.shape[1]), const(w_out.shape), row(d), const((1, d))],
        out_specs=row(d),
        compiler_params=_params("parallel"),
        name="out_proj",
    )(a, w_out, x, g)


def _ffn_kernel(x_ref, g1_ref, w1_ref, w2_ref, g2_ref, o_ref, h_ref, acc_ref):
    f = pl.program_id(1)

    @pl.when(f == 0)
    def _():
        x = x_ref[...]
        h_ref[...] = (x * _rms_scale(x) * g1_ref[...]).astype(h_ref.dtype)
        acc_ref[...] = jnp.zeros_like(acc_ref)

    t = jnp.maximum(jnp.dot(h_ref[...], w1_ref[...], preferred_element_type=F32), 0.0)
    acc_ref[...] += jnp.dot((t * t).astype(BF16), w2_ref[...], preferred_element_type=F32)

    @pl.when(f == pl.num_programs(1) - 1)
    def _():
        _post_norm_residual(x_ref, acc_ref[...], g2_ref, o_ref)


def _ffn(x, g1, w1, w2, g2, *, tm=512, tf=1024):
    m, d = x.shape
    d_ff = w1.shape[1]
    return pl.pallas_call(
        _ffn_kernel,
        out_shape=jax.ShapeDtypeStruct((m, d), F32),
        grid=(m // tm, d_ff // tf),
        in_specs=[
            pl.BlockSpec((tm, d), lambda i, f: (i, 0)),
            pl.BlockSpec((1, d), lambda i, f: (0, 0)),
            pl.BlockSpec((d, tf), lambda i, f: (0, f)),
            pl.BlockSpec((tf, d), lambda i, f: (f, 0)),
            pl.BlockSpec((1, d), lambda i, f: (0, 0)),
        ],
        out_specs=pl.BlockSpec((tm, d), lambda i, f: (i, 0)),
        scratch_shapes=[pltpu.VMEM((tm, d), BF16), pltpu.VMEM((tm, d), F32)],
        compiler_params=_params("parallel", "arbitrary"),
        name="ffn",
    )(x, g1, w1, w2, g2)


SB_ZERO_SUFFIX = 105.0


def _sb_kernel(q_ref, k_ref, v_ref, tri_ref, o_ref, *, blk, heads, scale):
    i = pl.program_id(2)
    tri = tri_ref[...]
    row = lax.broadcasted_iota(jnp.int32, (blk, blk), 0)
    col = lax.broadcasted_iota(jnp.int32, (blk, blk), 1)
    causal = col < row
    no_suffix = jnp.zeros((blk, 1), F32)

    def block(h, j, suffix, diagonal):
        sl = slice(h * HEAD_DIM, (h + 1) * HEAD_DIM)
        start = pl.multiple_of(j * blk, blk)
        k = k_ref[pl.ds(start, blk), sl]
        v = v_ref[pl.ds(start, blk), sl]
        z = lax.dot_general(q_ref[:, sl], k, (((1,), (1,)), ((), ())), preferred_element_type=F32) * scale
        sp = jnp.maximum(z, 0.0) + jnp.log(1.0 + jnp.exp(-jnp.abs(z)))
        if diagonal:
            sp = jnp.where(causal, sp, 0.0)
        hi = sp.astype(BF16)
        lo = (sp - hi.astype(F32)).astype(BF16)
        cum = (jnp.dot(hi, tri, preferred_element_type=F32)
               + jnp.dot(lo, tri, preferred_element_type=F32) + suffix)
        a = jnp.exp(z - cum)
        if diagonal:
            a = jnp.where(causal, a, 0.0)
        return jnp.dot(a.astype(BF16), v, preferred_element_type=F32), cum[:, 0:1]

    def store(accs):
        for h in range(heads):
            o_ref[:, h * HEAD_DIM:(h + 1) * HEAD_DIM] = accs[h].astype(o_ref.dtype)

    @pl.when(i == 0)
    def _():
        store([block(h, 0, no_suffix, True)[0] for h in range(heads)])

    @pl.when(i > 0)
    def _():
        accs, sufs = [], []
        for h in range(heads):
            acc_d, suf_d = block(h, i, no_suffix, True)
            acc_p, suf_p = block(h, i - 1, suf_d, False)
            accs.append(acc_d + acc_p)
            sufs.append(suf_p)

        def cond(carry):
            j, _, sufs = carry
            smallest = functools.reduce(jnp.minimum, sufs)
            return jnp.logical_and(j >= 0, jnp.min(smallest) < SB_ZERO_SUFFIX)

        def body(carry):
            j, accs, sufs = carry
            new = [block(h, j, sufs[h], False) for h in range(heads)]
            return j - 1, tuple(accs[h] + new[h][0] for h in range(heads)), tuple(n[1] for n in new)

        _, accs, _ = lax.while_loop(cond, body, (i - 2, tuple(accs), tuple(sufs)))
        store(accs)


def _stick_breaking(qkv, bsz, t, *, n_heads, blk=256, heads=2):
    nq = t // blk
    groups = n_heads // heads
    width = heads * HEAD_DIM
    ar = jnp.arange(blk)
    tri = (ar[:, None] >= ar[None, :]).astype(BF16)
    return pl.pallas_call(
        functools.partial(_sb_kernel, blk=blk, heads=heads, scale=HEAD_DIM ** -0.5),
        out_shape=jax.ShapeDtypeStruct((bsz * t, n_heads * HEAD_DIM), BF16),
        grid=(bsz, groups, nq),
        in_specs=[
            pl.BlockSpec((blk, width), lambda b, g, i: (b * nq + i, g)),
            pl.BlockSpec((t, width), lambda b, g, i: (b, groups + g)),
            pl.BlockSpec((t, width), lambda b, g, i: (b, 2 * groups + g)),
            pl.BlockSpec((blk, blk), lambda b, g, i: (0, 0)),
        ],
        out_specs=pl.BlockSpec((blk, width), lambda b, g, i: (b * nq + i, g)),
        compiler_params=_params("parallel", "parallel", "arbitrary"),
        name="stick_breaking",
    )(qkv, qkv, qkv, tri)


def kernel(x, norm_pre_mix, norm_post_mix, norm_pre_ffn, norm_post_ffn, ab_w_in, sgu_ln_g, sgu_ln_b, sgu_w, sgu_b,
           ab_w_out, sb_w_in, sb_w_out, ffn_w1, ffn_w2):
    bsz, t, d = x.shape
    a_width = sgu_ln_g.shape[-1]
    b_heads = (ab_w_in.shape[-1] - 2 * a_width) // (3 * HEAD_DIM)
    c_heads = sb_w_in.shape[-1] // (3 * HEAD_DIM)
    assert a_width == b_heads * HEAD_DIM, "dilated q/k/v column blocks are addressed in units of a_width"

    xs = x.reshape(bsz * t, d)
    row = lambda v: v.reshape(1, -1).astype(F32)

    z = _norm_matmul(xs, row(norm_pre_mix[0]), ab_w_in[0].astype(BF16), row(sgu_ln_g[0]), row(sgu_ln_b[0]),
                     sgu_epilogue=True, tn=a_width)
    bias_full = jnp.repeat(sgu_b[0].T.astype(F32), CHUNK, axis=1)
    a_out = _sgu(z, sgu_w[0], bias_full, a_width=a_width)
    outs, lses = zip(*[_dilated_branch(z, bsz, t, window, dilation, n_heads=b_heads, q_col=2)
                       for window, dilation in DILATED_PAIRS])
    xs = _out_merge(a_out, outs, lses, ab_w_out[0].astype(BF16), xs, row(norm_post_mix[0]), n_heads=b_heads)
    xs = _ffn(xs, row(norm_pre_ffn[0]), ffn_w1[0].astype(BF16), ffn_w2[0].astype(BF16), row(norm_post_ffn[0]))

    zeros = jnp.zeros((1, 1024), F32)
    qkv = _norm_matmul(xs, row(norm_pre_mix[1]), sb_w_in[0].astype(BF16), zeros, zeros, sgu_epilogue=False)
    att = _stick_breaking(qkv, bsz, t, n_heads=c_heads)
    xs = _out_proj(att, sb_w_out[0].astype(BF16), xs, row(norm_post_mix[1]))
    xs = _ffn(xs, row(norm_pre_ffn[1]), ffn_w1[1].astype(BF16), ffn_w2[1].astype(BF16), row(norm_post_ffn[1]))
    return xs.reshape(bsz, t, d)
```

```python
import functools
import math

import jax
import jax.numpy as jnp
from jax import lax
from jax.experimental import pallas as pl
from jax.experimental.pallas import tpu as pltpu

HEAD_DIM = 128
CHUNK = 128
ATT_BLOCK = 128
DILATED_PAIRS = ((128, 1), (512, 4), (2048, 16))
RMS_EPS = 1e-6
LN_EPS = 1e-5
NEG_BIG = -1e30

V7X_VMEM_LIMIT_BYTES = 56 * 1024 * 1024

BF16 = jnp.bfloat16
F32 = jnp.float32


def _params(*semantics):
    return pltpu.CompilerParams(dimension_semantics=semantics, vmem_limit_bytes=V7X_VMEM_LIMIT_BYTES)


def _const_spec(shape):
    return pl.BlockSpec(shape, lambda i: (0,) * len(shape), pipeline_mode=pl.Buffered(1))


def _cast_kernel(w_ref, o_ref):
    o_ref[...] = w_ref[...].astype(o_ref.dtype)


def _to_bf16(w, layer, *, block_bytes=8 * 1024 * 1024):
    _, r, c = w.shape
    br = r
    while 4 * br * c > block_bytes and br % 32 == 0:
        br //= 2
    return pl.pallas_call(
        _cast_kernel,
        out_shape=jax.ShapeDtypeStruct((r, c), BF16),
        grid=(r // br,),
        in_specs=[pl.BlockSpec((None, br, c), lambda i: (layer, i, 0))],
        out_specs=pl.BlockSpec((br, c), lambda i: (i, 0)),
        compiler_params=_params("parallel"),
        name="to_bf16",
    )(w)


def _rms_scale(x):
    return lax.rsqrt(jnp.mean(x * x, axis=-1, keepdims=True) + RMS_EPS)


def _gelu(x):
    return 0.5 * x * (1.0 + lax.erf(x * math.sqrt(0.5)))


def _normed_row_chunks(x_ref, g_ref, rc):
    for c in range(x_ref.shape[0] // rc):
        rows = slice(c * rc, (c + 1) * rc)
        x = x_ref[rows, :]
        yield c, rows, (x * _rms_scale(x) * g_ref[...]).astype(BF16)


def _norm_matmul_kernel(x_ref, g_ref, w_ref, o_ref, *, rc, tn):
    for _, rows, h in _normed_row_chunks(x_ref, g_ref, rc):
        for n in range(o_ref.shape[1] // tn):
            cols = slice(n * tn, (n + 1) * tn)
            o_ref[rows, cols] = jnp.dot(h, w_ref[:, cols], preferred_element_type=F32).astype(o_ref.dtype)


def _norm_matmul(x, g, w, *, tm=512, rc=256, tn=1024):
    m, d = x.shape
    n = w.shape[1]
    return pl.pallas_call(
        functools.partial(_norm_matmul_kernel, rc=rc, tn=tn),
        out_shape=jax.ShapeDtypeStruct((m, n), BF16),
        grid=(m // tm,),
        in_specs=[pl.BlockSpec((tm, d), lambda i: (i, 0)), _const_spec((1, d)), _const_spec((d, n))],
        out_specs=pl.BlockSpec((tm, n), lambda i: (i, 0)),
        compiler_params=_params("parallel"),
        name="norm_matmul",
    )(x, g, w)


def _proj_uv_kernel(x_ref, g_ref, w_ref, lng_ref, lnb_ref, uv_ref, *, rc):
    tn = lng_ref.shape[1]
    for _, rows, h in _normed_row_chunks(x_ref, g_ref, rc):
        v = _gelu(jnp.dot(h, w_ref[:, tn:], preferred_element_type=F32))
        mu = jnp.mean(v, axis=-1, keepdims=True)
        c = v - mu
        var = jnp.mean(c * c, axis=-1, keepdims=True)
        uv_ref[rows, tn:] = (c * lax.rsqrt(var + LN_EPS) * lng_ref[...] + lnb_ref[...]).astype(uv_ref.dtype)
        u = jnp.dot(h, w_ref[:, :tn], preferred_element_type=F32)
        uv_ref[rows, :tn] = _gelu(u).astype(uv_ref.dtype)


def _proj_uv(x, g, w, ln_g, ln_b, *, tm=512, rc=256):
    m, d = x.shape
    n = 2 * ln_g.shape[1]
    return pl.pallas_call(
        functools.partial(_proj_uv_kernel, rc=rc),
        out_shape=jax.ShapeDtypeStruct((m, n), BF16),
        grid=(m // tm,),
        in_specs=[pl.BlockSpec((tm, d), lambda i: (i, 0)), _const_spec((1, d)), _const_spec((d, n)),
                  _const_spec(ln_g.shape), _const_spec(ln_b.shape)],
        out_specs=pl.BlockSpec((tm, n), lambda i: (i, 0)),
        compiler_params=_params("parallel"),
        name="proj_uv",
    )(x, g, w, ln_g, ln_b)


def _proj_qkv0_kernel(x_ref, g_ref, wq_ref, wk_ref, wv_ref, z1_ref, z4_ref, z16_ref, s_ref, s4_ref, *, rc):
    tn = wq_ref.shape[1]
    slabs = tn // 128
    q4, q16 = rc // 4, rc // 16
    for c, rows, h in _normed_row_chunks(x_ref, g_ref, rc):
        rows4 = slice(c * q4, (c + 1) * q4)
        rows16 = slice(c * q16, (c + 1) * q16)
        for part, w_ref in enumerate((wq_ref, wk_ref, wv_ref)):
            acc = jnp.dot(h, w_ref[...], preferred_element_type=F32)
            z1_ref[rows, part * tn:(part + 1) * tn] = acc.astype(z1_ref.dtype)
            for s in range(slabs):
                s_ref[s] = acc[:, s * 128:(s + 1) * 128]
            for r in range(4):
                for s in range(slabs):
                    col = (part * 4 + r) * tn + s * 128
                    piece = s_ref[s, pl.ds(r, q4, stride=4), :]
                    z4_ref[rows4, col:col + 128] = piece.astype(z4_ref.dtype)
                    s4_ref[s, r * q4:(r + 1) * q4, :] = piece
            for r in range(16):
                for s in range(slabs):
                    col = (part * 16 + r) * tn + s * 128
                    piece = s4_ref[s, pl.ds((r % 4) * q4 + r // 4, q16, stride=4), :]
                    z16_ref[rows16, col:col + 128] = piece.astype(z16_ref.dtype)


def _proj_qkv0(x, g, w, bsz, t, *, tn, first_block, tm=512, rc=256):
    m, d = x.shape
    n = 3 * tn
    tiles = t // tm
    grouped = lambda dil: pl.BlockSpec((None, tm // dil, dil * n), lambda i: (i // tiles, i % tiles, 0))
    w_block = lambda part: pl.BlockSpec((d, tn), lambda i: (0, first_block + part), pipeline_mode=pl.Buffered(1))
    return pl.pallas_call(
        functools.partial(_proj_qkv0_kernel, rc=rc),
        out_shape=tuple(jax.ShapeDtypeStruct((bsz, t // dil, dil * n), BF16) for dil in (1, 4, 16)),
        grid=(m // tm,),
        in_specs=[pl.BlockSpec((tm, d), lambda i: (i, 0)), _const_spec((1, d)), w_block(0), w_block(1), w_block(2)],
        out_specs=(grouped(1), grouped(4), grouped(16)),
        scratch_shapes=[pltpu.VMEM((tn // 128, rc, 128), F32), pltpu.VMEM((tn // 128, rc, 128), F32)],
        compiler_params=_params("parallel"),
        name="proj_qkv0",
    )(x, g, w, w, w)


def _sgu_kernel(u_ref, v_ref, w_ref, b_ref, o_ref, *, chunks, groups):
    row = lax.broadcasted_iota(jnp.int32, (CHUNK, CHUNK), 0)
    col = lax.broadcasted_iota(jnp.int32, (CHUNK, CHUNK), 1)
    for g in range(groups):
        cs = slice(g * CHUNK, (g + 1) * CHUNK)
        wg = jnp.where(row >= col, w_ref[g], 0.0).astype(BF16)
        bias = b_ref[:, cs]
        for c in range(chunks):
            rs = slice(c * CHUNK, (c + 1) * CHUNK)
            mixed = jnp.dot(wg, v_ref[rs, cs], preferred_element_type=F32) + bias
            o_ref[rs, cs] = (u_ref[rs, cs].astype(F32) * mixed).astype(o_ref.dtype)


def _sgu(z, w_s, bias_full, *, a_width, chunks=4):
    m = z.shape[0]
    groups = w_s.shape[0]
    tm = chunks * CHUNK
    return pl.pallas_call(
        functools.partial(_sgu_kernel, chunks=chunks, groups=groups),
        out_shape=jax.ShapeDtypeStruct((m, a_width), BF16),
        grid=(m // tm,),
        in_specs=[
            pl.BlockSpec((tm, a_width), lambda i: (i, 0)),
            pl.BlockSpec((tm, a_width), lambda i: (i, 1)),
            pl.BlockSpec((groups, CHUNK, CHUNK), lambda i: (0, 0, 0)),
            pl.BlockSpec((CHUNK, a_width), lambda i: (0, 0)),
        ],
        out_specs=pl.BlockSpec((tm, a_width), lambda i: (i, 0)),
        compiler_params=_params("parallel"),
        name="sgu",
    )(z, z, w_s, bias_full)


def _dilated_kernel(q_ref, kp_ref, kc_ref, vp_ref, vc_ref, bias_ref, o_ref, lse_ref, *, n_heads, dilation, scale):
    first = jnp.where(pl.program_id(1) == 0, 0, 1)
    rows = pl.ds(pl.program_id(2), ATT_BLOCK, stride=dilation) if dilation > 1 else slice(None)
    lane = lax.broadcasted_iota(jnp.int32, (ATT_BLOCK, 128), 1)
    lse = jnp.zeros((ATT_BLOCK, 128), F32)
    for h in range(n_heads):
        sl = slice(h * HEAD_DIM, (h + 1) * HEAD_DIM)
        k = jnp.concatenate([kp_ref[:, sl], kc_ref[:, sl]], axis=0)
        v = jnp.concatenate([vp_ref[:, sl], vc_ref[:, sl]], axis=0)
        s = lax.dot_general(q_ref[:, sl], k, (((1,), (1,)), ((), ())), preferred_element_type=F32)
        s = s * scale + bias_ref[first, h]
        m = jnp.max(s, axis=-1, keepdims=True)
        p = jnp.exp(s - m)
        den = jnp.sum(p, axis=-1, keepdims=True)
        o_ref[h, rows, :] = jnp.dot(p.astype(BF16), v, preferred_element_type=F32) * (1.0 / den)
        lse = jnp.where(lane == h, m + jnp.log(den), lse)
    lse_ref[rows, :] = lse


def _dilated_bias(window, dilation, n_heads):
    blk = ATT_BLOCK
    win_sub = window // dilation
    qi = jnp.arange(blk)[:, None]
    kj = jnp.arange(2 * blk)[None, :]
    dist = qi + blk - kj
    band = (dist >= 0) & (dist <= win_sub)
    slopes = jnp.exp2(-8.0 * (jnp.arange(n_heads, dtype=F32) + 1.0) / n_heads)
    bias = -slopes[:, None, None] * (dist * dilation).astype(F32)[None]
    later = jnp.where(band[None], bias, NEG_BIG)
    first = jnp.where((band & (kj >= blk))[None], bias, NEG_BIG)
    return jnp.stack([first, later], axis=0)


def _dilated_branch(zd, t, window, dilation, *, n_heads):
    bsz = zd.shape[0]
    width = n_heads * HEAD_DIM
    d = dilation
    nb = t // (d * ATT_BLOCK)
    period = d * ATT_BLOCK
    bias = _dilated_bias(window, d, n_heads)

    def cur(part):
        return pl.BlockSpec((None, ATT_BLOCK, width), lambda b, n, r: (b, n, part * d + r))

    def prev(part):
        return pl.BlockSpec((None, ATT_BLOCK, width), lambda b, n, r: (b, jnp.maximum(n - 1, 0), part * d + r))

    return pl.pallas_call(
        functools.partial(_dilated_kernel, n_heads=n_heads, dilation=d, scale=HEAD_DIM ** -0.5),
        out_shape=(jax.ShapeDtypeStruct((bsz, n_heads, t, HEAD_DIM), F32),
                   jax.ShapeDtypeStruct((bsz, t, 128), F32)),
        grid=(bsz, nb, d),
        in_specs=[cur(0), prev(1), cur(1), prev(2), cur(2),
                  pl.BlockSpec((2, n_heads, ATT_BLOCK, 2 * ATT_BLOCK), lambda b, n, r: (0, 0, 0, 0))],
        out_specs=(pl.BlockSpec((None, n_heads, period, HEAD_DIM), lambda b, n, r: (b, 0, n, 0)),
                   pl.BlockSpec((None, period, 128), lambda b, n, r: (b, n, 0))),
        compiler_params=_params("parallel", "arbitrary", "arbitrary"),
        name=f"dilated_d{d}",
    )(zd, zd, zd, zd, zd, bias)


def _post_norm_residual(x_ref, y, g_ref, o_ref):
    o_ref[...] = x_ref[...] + y * _rms_scale(y) * g_ref[...]


def _out_merge_kernel(a_ref, o1_ref, o2_ref, o3_ref, l1_ref, l2_ref, l3_ref, w_ref, x_ref, g_ref, out_ref,
                      *, n_heads, a_width):
    l1, l2, l3 = l1_ref[...], l2_ref[...], l3_ref[...]
    m = jnp.maximum(jnp.maximum(l1, l2), l3)
    e1, e2, e3 = jnp.exp(l1 - m), jnp.exp(l2 - m), jnp.exp(l3 - m)
    inv = 1.0 / (e1 + e2 + e3)
    w1, w2, w3 = e1 * inv, e2 * inv, e3 * inv
    parts = []
    for h in range(n_heads):
        bh = w1[:, h:h + 1] * o1_ref[h] + w2[:, h:h + 1] * o2_ref[h] + w3[:, h:h + 1] * o3_ref[h]
        parts.append(bh.astype(BF16))
    b_out = jnp.concatenate(parts, axis=1)
    y = (jnp.dot(a_ref[...], w_ref[:a_width, :], preferred_element_type=F32)
         + jnp.dot(b_out, w_ref[a_width:, :], preferred_element_type=F32))
    _post_norm_residual(x_ref, y, g_ref, out_ref)


def _out_merge(a_out, outs, lses, w_out, x, g, *, tm=512):
    m, d = x.shape
    a_width = a_out.shape[1]
    bsz, n_heads, t, _ = outs[0].shape
    tiles = t // tm
    row = lambda width: pl.BlockSpec((tm, width), lambda i: (i, 0))
    heads = pl.BlockSpec((None, n_heads, tm, HEAD_DIM), lambda i: (i // tiles, 0, i % tiles, 0))
    return pl.pallas_call(
        functools.partial(_out_merge_kernel, n_heads=n_heads, a_width=a_width),
        out_shape=jax.ShapeDtypeStruct((m, d), F32),
        grid=(m // tm,),
        in_specs=[row(a_width), heads, heads, heads, row(128), row(128), row(128),
                  _const_spec(w_out.shape), row(d), _const_spec((1, d))],
        out_specs=row(d),
        compiler_params=_params("parallel"),
        name="out_merge",
    )(a_out, *outs, *lses, w_out, x, g)


def _out_proj_kernel(a_ref, w_ref, x_ref, g_ref, out_ref):
    y = jnp.dot(a_ref[...], w_ref[...], preferred_element_type=F32)
    _post_norm_residual(x_ref, y, g_ref, out_ref)


def _out_proj(a, w_out, x, g, *, tm=512):
    m, d = x.shape
    row = lambda width: pl.BlockSpec((tm, width), lambda i: (i, 0))
    return pl.pallas_call(
        _out_proj_kernel,
        out_shape=jax.ShapeDtypeStruct((m, d), F32),
        grid=(m // tm,),
        in_specs=[row(a.shape[1]), _const_spec(w_out.shape), row(d), _const_spec((1, d))],
        out_specs=row(d),
        compiler_params=_params("parallel"),
        name="out_proj",
    )(a, w_out, x, g)


def _ffn_kernel(x_ref, g1_ref, w1_ref, w2_ref, g2_ref, o_ref, h_ref, acc_ref):
    f = pl.program_id(1)

    @pl.when(f == 0)
    def _():
        x = x_ref[...]
        h_ref[...] = (x * _rms_scale(x) * g1_ref[...]).astype(h_ref.dtype)
        acc_ref[...] = jnp.zeros_like(acc_ref)

    t = jnp.maximum(jnp.dot(h_ref[...], w1_ref[...], preferred_element_type=F32), 0.0)
    acc_ref[...] += jnp.dot((t * t).astype(BF16), w2_ref[...], preferred_element_type=F32)

    @pl.when(f == pl.num_programs(1) - 1)
    def _():
        _post_norm_residual(x_ref, acc_ref[...], g2_ref, o_ref)


def _ffn(x, g1, w1, w2, g2, *, tm=512, tf=1024):
    m, d = x.shape
    d_ff = w1.shape[1]
    return pl.pallas_call(
        _ffn_kernel,
        out_shape=jax.ShapeDtypeStruct((m, d), F32),
        grid=(m // tm, d_ff // tf),
        in_specs=[
            pl.BlockSpec((tm, d), lambda i, f: (i, 0)),
            pl.BlockSpec((1, d), lambda i, f: (0, 0)),
            pl.BlockSpec((d, tf), lambda i, f: (0, f)),
            pl.BlockSpec((tf, d), lambda i, f: (f, 0)),
            pl.BlockSpec((1, d), lambda i, f: (0, 0)),
        ],
        out_specs=pl.BlockSpec((tm, d), lambda i, f: (i, 0)),
        scratch_shapes=[pltpu.VMEM((tm, d), BF16), pltpu.VMEM((tm, d), F32)],
        compiler_params=_params("parallel", "arbitrary"),
        name="ffn",
    )(x, g1, w1, w2, g2)


LOG2_E = 1.4426950408889634
SB_ZERO_SUFFIX_BITS = 152.0


def _sb_kernel(q_ref, k_ref, v_ref, tri_ref, o_ref, *, blk, heads, scale):
    i = pl.program_id(2)
    tri = tri_ref[...]
    row = lax.broadcasted_iota(jnp.int32, (blk, blk), 0)
    col = lax.broadcasted_iota(jnp.int32, (blk, blk), 1)
    causal = col < row
    no_suffix = jnp.zeros((blk, 1), F32)

    def block(h, j, suffix, diagonal):
        sl = slice(h * HEAD_DIM, (h + 1) * HEAD_DIM)
        start = pl.multiple_of(j * blk, blk)
        k = k_ref[pl.ds(start, blk), sl]
        v = v_ref[pl.ds(start, blk), sl]
        z2 = lax.dot_general(q_ref[:, sl], k, (((1,), (1,)), ((), ())),
                             preferred_element_type=F32) * (scale * LOG2_E)
        sp = jnp.maximum(z2, 0.0) + jnp.log(1.0 + jnp.exp2(-jnp.abs(z2))) * LOG2_E
        if diagonal:
            sp = jnp.where(causal, sp, 0.0)
        cum = jnp.dot(sp.astype(BF16), tri, preferred_element_type=F32) + suffix
        a = jnp.exp2(z2 - cum)
        if diagonal:
            a = jnp.where(causal, a, 0.0)
        return jnp.dot(a.astype(BF16), v, preferred_element_type=F32), cum[:, 0:1]

    def store(accs):
        for h in range(heads):
            o_ref[:, h * HEAD_DIM:(h + 1) * HEAD_DIM] = accs[h].astype(o_ref.dtype)

    @pl.when(i == 0)
    def _():
        store([block(h, 0, no_suffix, True)[0] for h in range(heads)])

    @pl.when(i > 0)
    def _():
        accs, sufs = [], []
        for h in range(heads):
            acc_d, suf_d = block(h, i, no_suffix, True)
            acc_p, suf_p = block(h, i - 1, suf_d, False)
            accs.append(acc_d + acc_p)
            sufs.append(suf_p)

        def cond(carry):
            j, _, sufs = carry
            smallest = functools.reduce(jnp.minimum, sufs)
            return jnp.logical_and(j >= 0, jnp.min(smallest) < SB_ZERO_SUFFIX_BITS)

        def body(carry):
            j, accs, sufs = carry
            new = [block(h, j, sufs[h], False) for h in range(heads)]
            return j - 1, tuple(accs[h] + new[h][0] for h in range(heads)), tuple(n[1] for n in new)

        _, accs, _ = lax.while_loop(cond, body, (i - 2, tuple(accs), tuple(sufs)))
        store(accs)


def _stick_breaking(qkv, bsz, t, *, n_heads, blk=256, heads=4):
    nq = t // blk
    groups = n_heads // heads
    width = heads * HEAD_DIM
    ar = jnp.arange(blk)
    tri = (ar[:, None] >= ar[None, :]).astype(BF16)
    return pl.pallas_call(
        functools.partial(_sb_kernel, blk=blk, heads=heads, scale=HEAD_DIM ** -0.5),
        out_shape=jax.ShapeDtypeStruct((bsz * t, n_heads * HEAD_DIM), BF16),
        grid=(bsz, groups, nq),
        in_specs=[
            pl.BlockSpec((blk, width), lambda b, g, i: (b * nq + i, g)),
            pl.BlockSpec((t, width), lambda b, g, i: (b, groups + g)),
            pl.BlockSpec((t, width), lambda b, g, i: (b, 2 * groups + g)),
            pl.BlockSpec((blk, blk), lambda b, g, i: (0, 0)),
        ],
        out_specs=pl.BlockSpec((blk, width), lambda b, g, i: (b * nq + i, g)),
        compiler_params=_params("parallel", "parallel", "arbitrary"),
        name="stick_breaking",
    )(qkv, qkv, qkv, tri)


def kernel(x, norm_pre_mix, norm_post_mix, norm_pre_ffn, norm_post_ffn, ab_w_in, sgu_ln_g, sgu_ln_b, sgu_w, sgu_b,
           ab_w_out, sb_w_in, sb_w_out, ffn_w1, ffn_w2):
    bsz, t, d = x.shape
    a_width = sgu_ln_g.shape[-1]
    b_heads = (ab_w_in.shape[-1] - 2 * a_width) // (3 * HEAD_DIM)
    c_heads = sb_w_in.shape[-1] // (3 * HEAD_DIM)
    assert a_width == b_heads * HEAD_DIM, "dilated q/k/v column blocks are addressed in units of a_width"

    xs = x.reshape(bsz * t, d)
    row = lambda v: v.reshape(1, -1).astype(F32)

    assert tuple(dil for _, dil in DILATED_PAIRS) == (1, 4, 16), "proj_qkv0 regroups q/k/v for dilations 1, 4, 16"
    w_in = _to_bf16(ab_w_in, 0)
    uv = _proj_uv(xs, row(norm_pre_mix[0]), w_in, row(sgu_ln_g[0]), row(sgu_ln_b[0]))
    grouped = _proj_qkv0(xs, row(norm_pre_mix[0]), w_in, bsz, t, tn=a_width, first_block=2)
    bias_full = jnp.repeat(sgu_b[0].T.astype(F32), CHUNK, axis=1)
    a_out = _sgu(uv, sgu_w[0], bias_full, a_width=a_width)
    outs, lses = zip(*[_dilated_branch(zd, t, window, dilation, n_heads=b_heads)
                       for zd, (window, dilation) in zip(grouped, DILATED_PAIRS)])
    lses = [lse.reshape(bsz * t, 128) for lse in lses]
    xs = _out_merge(a_out, outs, lses, _to_bf16(ab_w_out, 0), xs, row(norm_post_mix[0]))
    xs = _ffn(xs, row(norm_pre_ffn[0]), _to_bf16(ffn_w1, 0), _to_bf16(ffn_w2, 0), row(norm_post_ffn[0]))

    qkv = _norm_matmul(xs, row(norm_pre_mix[1]), _to_bf16(sb_w_in, 0))
    att = _stick_breaking(qkv, bsz, t, n_heads=c_heads)
    xs = _out_proj(att, _to_bf16(sb_w_out, 0), xs, row(norm_post_mix[1]))
    xs = _ffn(xs, row(norm_pre_ffn[1]), _to_bf16(ffn_w1, 1), _to_bf16(ffn_w2, 1), row(norm_post_ffn[1]))
    return xs.reshape(bsz, t, d)
```

```python
import functools
import math

import jax
import jax.numpy as jnp
from jax import lax
from jax.experimental import pallas as pl
from jax.experimental.pallas import tpu as pltpu

HEAD_DIM = 128
CHUNK = 128
ATT_BLOCK = 128
DILATED_PAIRS = ((128, 1), (512, 4), (2048, 16))
DILATED_STEP = ((8, 1), (2, 4), (1, 4))
RMS_EPS = 1e-6
LN_EPS = 1e-5
NEG_BIG = -1e30
LOG2_E = 1.4426950408889634

V7X_VMEM_LIMIT_BYTES = 56 * 1024 * 1024

BF16 = jnp.bfloat16
F32 = jnp.float32


def _params(*semantics):
    return pltpu.CompilerParams(dimension_semantics=semantics, vmem_limit_bytes=V7X_VMEM_LIMIT_BYTES)


def _const_spec(shape):
    return pl.BlockSpec(shape, lambda i: (0,) * len(shape), pipeline_mode=pl.Buffered(1))


def _cast_kernel(w_ref, o_ref):
    o_ref[...] = w_ref[...].astype(o_ref.dtype)


def _to_bf16(w, layer, *, block_bytes=8 * 1024 * 1024):
    _, r, c = w.shape
    br = r
    while 4 * br * c > block_bytes and br % 32 == 0:
        br //= 2
    return pl.pallas_call(
        _cast_kernel,
        out_shape=jax.ShapeDtypeStruct((r, c), BF16),
        grid=(r // br,),
        in_specs=[pl.BlockSpec((None, br, c), lambda i: (layer, i, 0))],
        out_specs=pl.BlockSpec((br, c), lambda i: (i, 0)),
        compiler_params=_params("parallel"),
        name="to_bf16",
    )(w)


def _rms_scale(x):
    return lax.rsqrt(jnp.mean(x * x, axis=-1, keepdims=True) + RMS_EPS)


def _gelu(x):
    return 0.5 * x * (1.0 + lax.erf(x * math.sqrt(0.5)))


def _normed_row_chunks(x_ref, g_ref, rc):
    for c in range(x_ref.shape[0] // rc):
        rows = slice(c * rc, (c + 1) * rc)
        x = x_ref[rows, :]
        yield c, rows, (x * _rms_scale(x) * g_ref[...]).astype(BF16)


def _norm_matmul_kernel(x_ref, g_ref, w_ref, o_ref, *, rc, tn):
    for _, rows, h in _normed_row_chunks(x_ref, g_ref, rc):
        for n in range(o_ref.shape[1] // tn):
            cols = slice(n * tn, (n + 1) * tn)
            o_ref[rows, cols] = jnp.dot(h, w_ref[:, cols], preferred_element_type=F32).astype(o_ref.dtype)


def _norm_matmul(x, g, w, *, tm=512, rc=256, tn=1024):
    m, d = x.shape
    n = w.shape[1]
    return pl.pallas_call(
        functools.partial(_norm_matmul_kernel, rc=rc, tn=tn),
        out_shape=jax.ShapeDtypeStruct((m, n), BF16),
        grid=(m // tm,),
        in_specs=[pl.BlockSpec((tm, d), lambda i: (i, 0)), _const_spec((1, d)), _const_spec((d, n))],
        out_specs=pl.BlockSpec((tm, n), lambda i: (i, 0)),
        compiler_params=_params("parallel"),
        name="norm_matmul",
    )(x, g, w)


def _proj_uv_kernel(x_ref, g_ref, w_ref, lng_ref, lnb_ref, uv_ref, *, rc):
    tn = lng_ref.shape[1]
    for _, rows, h in _normed_row_chunks(x_ref, g_ref, rc):
        v = _gelu(jnp.dot(h, w_ref[:, tn:], preferred_element_type=F32))
        mu = jnp.mean(v, axis=-1, keepdims=True)
        c = v - mu
        var = jnp.mean(c * c, axis=-1, keepdims=True)
        uv_ref[rows, tn:] = (c * lax.rsqrt(var + LN_EPS) * lng_ref[...] + lnb_ref[...]).astype(uv_ref.dtype)
        u = jnp.dot(h, w_ref[:, :tn], preferred_element_type=F32)
        uv_ref[rows, :tn] = _gelu(u).astype(uv_ref.dtype)


def _proj_uv(x, g, w, ln_g, ln_b, *, tm=512, rc=256):
    m, d = x.shape
    n = 2 * ln_g.shape[1]
    return pl.pallas_call(
        functools.partial(_proj_uv_kernel, rc=rc),
        out_shape=jax.ShapeDtypeStruct((m, n), BF16),
        grid=(m // tm,),
        in_specs=[pl.BlockSpec((tm, d), lambda i: (i, 0)), _const_spec((1, d)), _const_spec((d, n)),
                  _const_spec(ln_g.shape), _const_spec(ln_b.shape)],
        out_specs=pl.BlockSpec((tm, n), lambda i: (i, 0)),
        compiler_params=_params("parallel"),
        name="proj_uv",
    )(x, g, w, ln_g, ln_b)


def _proj_qkv0_kernel(x_ref, g_ref, wq_ref, wk_ref, wv_ref, z1_ref, z4_ref, z16_ref, s_ref, s4_ref, *, rc, q_scale):
    tn = wq_ref.shape[1]
    slabs = tn // 128
    q4, q16 = rc // 4, rc // 16
    for c, rows, h in _normed_row_chunks(x_ref, g_ref, rc):
        rows4 = slice(c * q4, (c + 1) * q4)
        rows16 = slice(c * q16, (c + 1) * q16)
        for part, w_ref in enumerate((wq_ref, wk_ref, wv_ref)):
            acc = jnp.dot(h, w_ref[...], preferred_element_type=F32)
            if part == 0:
                acc = acc * q_scale
            z1_ref[rows, part * tn:(part + 1) * tn] = acc.astype(z1_ref.dtype)
            for s in range(slabs):
                s_ref[s] = acc[:, s * 128:(s + 1) * 128]
            for r in range(4):
                for s in range(slabs):
                    col = (part * 4 + r) * tn + s * 128
                    piece = s_ref[s, pl.ds(r, q4, stride=4), :]
                    z4_ref[rows4, col:col + 128] = piece.astype(z4_ref.dtype)
                    s4_ref[s, r * q4:(r + 1) * q4, :] = piece
            for r in range(16):
                for s in range(slabs):
                    col = (part * 16 + r) * tn + s * 128
                    piece = s4_ref[s, pl.ds((r % 4) * q4 + r // 4, q16, stride=4), :]
                    z16_ref[rows16, col:col + 128] = piece.astype(z16_ref.dtype)


def _proj_qkv0(x, g, w, bsz, t, *, tn, first_block, q_scale, tm=512, rc=256):
    m, d = x.shape
    n = 3 * tn
    tiles = t // tm
    grouped = lambda dil: pl.BlockSpec((None, tm // dil, dil * n), lambda i: (i // tiles, i % tiles, 0))
    w_block = lambda part: pl.BlockSpec((d, tn), lambda i: (0, first_block + part), pipeline_mode=pl.Buffered(1))
    return pl.pallas_call(
        functools.partial(_proj_qkv0_kernel, rc=rc, q_scale=q_scale),
        out_shape=tuple(jax.ShapeDtypeStruct((bsz, t // dil, dil * n), BF16) for dil in (1, 4, 16)),
        grid=(m // tm,),
        in_specs=[pl.BlockSpec((tm, d), lambda i: (i, 0)), _const_spec((1, d)), w_block(0), w_block(1), w_block(2)],
        out_specs=(grouped(1), grouped(4), grouped(16)),
        scratch_shapes=[pltpu.VMEM((tn // 128, rc, 128), F32), pltpu.VMEM((tn // 128, rc, 128), F32)],
        compiler_params=_params("parallel"),
        name="proj_qkv0",
    )(x, g, w, w, w)


def _sgu_kernel(u_ref, v_ref, w_ref, b_ref, o_ref, *, chunks, groups):
    row = lax.broadcasted_iota(jnp.int32, (CHUNK, CHUNK), 0)
    col = lax.broadcasted_iota(jnp.int32, (CHUNK, CHUNK), 1)
    for g in range(groups):
        cs = slice(g * CHUNK, (g + 1) * CHUNK)
        wg = jnp.where(row >= col, w_ref[g], 0.0).astype(BF16)
        bias = b_ref[:, cs]
        for c in range(chunks):
            rs = slice(c * CHUNK, (c + 1) * CHUNK)
            mixed = jnp.dot(wg, v_ref[rs, cs], preferred_element_type=F32) + bias
            o_ref[rs, cs] = (u_ref[rs, cs].astype(F32) * mixed).astype(o_ref.dtype)


def _sgu(z, w_s, bias_full, *, a_width, chunks=4):
    m = z.shape[0]
    groups = w_s.shape[0]
    tm = chunks * CHUNK
    return pl.pallas_call(
        functools.partial(_sgu_kernel, chunks=chunks, groups=groups),
        out_shape=jax.ShapeDtypeStruct((m, a_width), BF16),
        grid=(m // tm,),
        in_specs=[
            pl.BlockSpec((tm, a_width), lambda i: (i, 0)),
            pl.BlockSpec((tm, a_width), lambda i: (i, 1)),
            pl.BlockSpec((groups, CHUNK, CHUNK), lambda i: (0, 0, 0)),
            pl.BlockSpec((CHUNK, a_width), lambda i: (0, 0)),
        ],
        out_specs=pl.BlockSpec((tm, a_width), lambda i: (i, 0)),
        compiler_params=_params("parallel"),
        name="sgu",
    )(z, z, w_s, bias_full)


def _dilated_kernel(q_ref, kp_ref, kc_ref, vp_ref, vc_ref, bias_ref, o_ref, lse_ref, *, n_heads, dilation, qb, rb):
    blk = ATT_BLOCK
    width = n_heads * HEAD_DIM
    at_start = pl.program_id(1) == 0
    lane = lax.broadcasted_iota(jnp.int32, (blk, 128), 1)
    ones = jnp.ones((2 * blk, HEAD_DIM), BF16)
    for u, rr in [(u, rr) for u in range(qb) for rr in range(rb)]:
        first = jnp.where(at_start, 0, 1) if u == 0 else 1
        start = u * blk * dilation + pl.program_id(2) * rb + rr
        rows = pl.ds(start, blk, stride=dilation) if dilation > 1 else slice(u * blk, (u + 1) * blk)
        cur = slice(u * blk, (u + 1) * blk)
        before = slice((u - 1) * blk, u * blk)
        lse = jnp.zeros((blk, 128), F32)
        for h in range(n_heads):
            sl = slice(rr * width + h * HEAD_DIM, rr * width + (h + 1) * HEAD_DIM)
            k_prev = kp_ref[:, sl] if u == 0 else kc_ref[before, sl]
            v_prev = vp_ref[:, sl] if u == 0 else vc_ref[before, sl]
            k = jnp.concatenate([k_prev, kc_ref[cur, sl]], axis=0)
            v = jnp.concatenate([v_prev, vc_ref[cur, sl]], axis=0)
            s = lax.dot_general(q_ref[cur, sl], k, (((1,), (1,)), ((), ())), preferred_element_type=F32)
            s = s + bias_ref[first, h]
            m = jnp.max(s, axis=-1, keepdims=True)
            p = jnp.exp2(s - m).astype(BF16)
            pv = jnp.dot(p, jnp.concatenate([v, ones], axis=1), preferred_element_type=F32)
            den = pv[:, HEAD_DIM:]
            o_ref[h, rows, :] = pv[:, :HEAD_DIM] * (1.0 / den)
            lse = jnp.where(lane == h, m + jnp.log(den) * LOG2_E, lse)
        lse_ref[rows, :] = lse


def _dilated_bias(window, dilation, n_heads):
    blk = ATT_BLOCK
    win_sub = window // dilation
    qi = jnp.arange(blk)[:, None]
    kj = jnp.arange(2 * blk)[None, :]
    dist = qi + blk - kj
    band = (dist >= 0) & (dist <= win_sub)
    slopes = jnp.exp2(-8.0 * (jnp.arange(n_heads, dtype=F32) + 1.0) / n_heads)
    bias = -slopes[:, None, None] * (dist * dilation).astype(F32)[None] * LOG2_E
    later = jnp.where(band[None], bias, NEG_BIG)
    first = jnp.where((band & (kj >= blk))[None], bias, NEG_BIG)
    return jnp.stack([first, later], axis=0)


def _dilated_branch(zd, t, window, dilation, *, n_heads, qb, rb):
    bsz = zd.shape[0]
    width = n_heads * HEAD_DIM
    d = dilation
    steps = t // (d * ATT_BLOCK * qb)
    span = qb * d * ATT_BLOCK
    groups = d // rb
    bias = _dilated_bias(window, d, n_heads)

    def cur(part):
        return pl.BlockSpec((None, qb * ATT_BLOCK, rb * width), lambda b, n, r: (b, n, part * groups + r))

    def prev(part):
        return pl.BlockSpec((None, ATT_BLOCK, rb * width),
                            lambda b, n, r: (b, jnp.maximum(qb * n - 1, 0), part * groups + r))

    return pl.pallas_call(
        functools.partial(_dilated_kernel, n_heads=n_heads, dilation=d, qb=qb, rb=rb),
        out_shape=(jax.ShapeDtypeStruct((bsz, n_heads, t, HEAD_DIM), F32),
                   jax.ShapeDtypeStruct((bsz, t, 128), F32)),
        grid=(bsz, steps, groups),
        in_specs=[cur(0), prev(1), cur(1), prev(2), cur(2),
                  pl.BlockSpec((2, n_heads, ATT_BLOCK, 2 * ATT_BLOCK), lambda b, n, r: (0, 0, 0, 0))],
        out_specs=(pl.BlockSpec((None, n_heads, span, HEAD_DIM), lambda b, n, r: (b, 0, n, 0)),
                   pl.BlockSpec((None, span, 128), lambda b, n, r: (b, n, 0))),
        compiler_params=_params("parallel", "arbitrary", "arbitrary"),
        name=f"dilated_d{d}",
    )(zd, zd, zd, zd, zd, bias)


def _post_norm_residual(x_ref, y, g_ref, o_ref):
    o_ref[...] = x_ref[...] + y * _rms_scale(y) * g_ref[...]


def _out_merge_kernel(a_ref, o1_ref, o2_ref, o3_ref, l1_ref, l2_ref, l3_ref, w_ref, x_ref, g_ref, out_ref,
                      *, n_heads, a_width, rc):
    for c in range(x_ref.shape[0] // rc):
        rows = slice(c * rc, (c + 1) * rc)
        l1, l2, l3 = l1_ref[rows, :], l2_ref[rows, :], l3_ref[rows, :]
        m = jnp.maximum(jnp.maximum(l1, l2), l3)
        e1, e2, e3 = jnp.exp2(l1 - m), jnp.exp2(l2 - m), jnp.exp2(l3 - m)
        inv = 1.0 / (e1 + e2 + e3)
        w1, w2, w3 = e1 * inv, e2 * inv, e3 * inv
        parts = []
        for h in range(n_heads):
            bh = (w1[:, h:h + 1] * o1_ref[h, rows, :] + w2[:, h:h + 1] * o2_ref[h, rows, :]
                  + w3[:, h:h + 1] * o3_ref[h, rows, :])
            parts.append(bh.astype(BF16))
        b_out = jnp.concatenate(parts, axis=1)
        y = (jnp.dot(a_ref[rows, :], w_ref[:a_width, :], preferred_element_type=F32)
             + jnp.dot(b_out, w_ref[a_width:, :], preferred_element_type=F32))
        out_ref[rows, :] = x_ref[rows, :] + y * _rms_scale(y) * g_ref[...]


def _out_merge(a_out, outs, lses, w_out, x, g, *, tm=512, rc=256):
    m, d = x.shape
    a_width = a_out.shape[1]
    bsz, n_heads, t, _ = outs[0].shape
    tiles = t // tm
    row = lambda width: pl.BlockSpec((tm, width), lambda i: (i, 0))
    heads = pl.BlockSpec((None, n_heads, tm, HEAD_DIM), lambda i: (i // tiles, 0, i % tiles, 0))
    return pl.pallas_call(
        functools.partial(_out_merge_kernel, n_heads=n_heads, a_width=a_width, rc=rc),
        out_shape=jax.ShapeDtypeStruct((m, d), F32),
        grid=(m // tm,),
        in_specs=[row(a_width), heads, heads, heads, row(128), row(128), row(128),
                  _const_spec(w_out.shape), row(d), _const_spec((1, d))],
        out_specs=row(d),
        compiler_params=_params("parallel"),
        name="out_merge",
    )(a_out, *outs, *lses, w_out, x, g)


def _out_proj_kernel(a_ref, w_ref, x_ref, g_ref, out_ref, *, rc):
    for c in range(x_ref.shape[0] // rc):
        rows = slice(c * rc, (c + 1) * rc)
        y = jnp.dot(a_ref[rows, :], w_ref[...], preferred_element_type=F32)
        out_ref[rows, :] = x_ref[rows, :] + y * _rms_scale(y) * g_ref[...]


def _out_proj(a, w_out, x, g, *, tm=512, rc=256):
    m, d = x.shape
    row = lambda width: pl.BlockSpec((tm, width), lambda i: (i, 0))
    return pl.pallas_call(
        functools.partial(_out_proj_kernel, rc=rc),
        out_shape=jax.ShapeDtypeStruct((m, d), F32),
        grid=(m // tm,),
        in_specs=[row(a.shape[1]), _const_spec(w_out.shape), row(d), _const_spec((1, d))],
        out_specs=row(d),
        compiler_params=_params("parallel"),
        name="out_proj",
    )(a, w_out, x, g)


def _ffn_kernel(x_ref, g1_ref, w1_ref, w2_ref, g2_ref, o_ref, h_ref, acc_ref, *, rc):
    f = pl.program_id(1)
    last = pl.num_programs(1) - 1
    chunks = [slice(c * rc, (c + 1) * rc) for c in range(x_ref.shape[0] // rc)]

    def mlp(h):
        t = jnp.maximum(jnp.dot(h, w1_ref[...], preferred_element_type=F32), 0.0)
        return jnp.dot((t * t).astype(BF16), w2_ref[...], preferred_element_type=F32)

    @pl.when(f == 0)
    def _():
        for rows in chunks:
            x = x_ref[rows, :]
            h = (x * _rms_scale(x) * g1_ref[...]).astype(h_ref.dtype)
            h_ref[rows, :] = h
            acc_ref[rows, :] = mlp(h)

    @pl.when(jnp.logical_and(f > 0, f < last))
    def _():
        for rows in chunks:
            acc_ref[rows, :] += mlp(h_ref[rows, :])

    @pl.when(f == last)
    def _():
        for rows in chunks:
            y = acc_ref[rows, :] + mlp(h_ref[rows, :])
            o_ref[rows, :] = x_ref[rows, :] + y * _rms_scale(y) * g2_ref[...]


def _ffn(x, g1, w1, w2, g2, *, tm=512, rc=256, tf=1024):
    m, d = x.shape
    d_ff = w1.shape[1]
    assert d_ff // tf >= 2, "the first and the last hidden slice are distinct grid steps"
    return pl.pallas_call(
        functools.partial(_ffn_kernel, rc=rc),
        out_shape=jax.ShapeDtypeStruct((m, d), F32),
        grid=(m // tm, d_ff // tf),
        in_specs=[
            pl.BlockSpec((tm, d), lambda i, f: (i, 0)),
            pl.BlockSpec((1, d), lambda i, f: (0, 0)),
            pl.BlockSpec((d, tf), lambda i, f: (0, f)),
            pl.BlockSpec((tf, d), lambda i, f: (f, 0)),
            pl.BlockSpec((1, d), lambda i, f: (0, 0)),
        ],
        out_specs=pl.BlockSpec((tm, d), lambda i, f: (i, 0)),
        scratch_shapes=[pltpu.VMEM((tm, d), BF16), pltpu.VMEM((tm, d), F32)],
        compiler_params=_params("parallel", "arbitrary"),
        name="ffn",
    )(x, g1, w1, w2, g2)


SB_ZERO_SUFFIX_BITS = 152.0


def _sb_kernel(q_ref, k_ref, v_ref, tri_ref, o_ref, *, blk, heads, scale):
    i = pl.program_id(2)
    tri = tri_ref[...]
    row = lax.broadcasted_iota(jnp.int32, (blk, blk), 0)
    col = lax.broadcasted_iota(jnp.int32, (blk, blk), 1)
    causal = col < row
    no_suffix = jnp.zeros((blk, 1), F32)

    def blocks(jobs):
        cols = [slice(h * HEAD_DIM, (h + 1) * HEAD_DIM) for h, _, _, _ in jobs]
        starts = [pl.multiple_of(j * blk, blk) for _, j, _, _ in jobs]
        z2s = [lax.dot_general(q_ref[:, sl], k_ref[pl.ds(st, blk), sl], (((1,), (1,)), ((), ())),
                               preferred_element_type=F32) * (scale * LOG2_E) for sl, st in zip(cols, starts)]
        sps = [jnp.maximum(z2, 0.0) + jnp.log(1.0 + jnp.exp2(-jnp.abs(z2))) * LOG2_E for z2 in z2s]
        sps = [jnp.where(causal, sp, 0.0) if job[2] else sp for sp, job in zip(sps, jobs)]
        sums = [jnp.dot(sp.astype(BF16), tri, preferred_element_type=F32) for sp in sps]
        cums = []
        for part, (_, _, _, suffix) in zip(sums, jobs):
            cums.append(part + (cums[suffix][:, 0:1] if isinstance(suffix, int) else suffix))
        weights = [jnp.exp2(z2 - cum) for z2, cum in zip(z2s, cums)]
        weights = [jnp.where(causal, a, 0.0) if job[2] else a for a, job in zip(weights, jobs)]
        outs = [jnp.dot(a.astype(BF16), v_ref[pl.ds(st, blk), sl], preferred_element_type=F32)
                for a, sl, st in zip(weights, cols, starts)]
        return [(o, cum[:, 0:1]) for o, cum in zip(outs, cums)]

    def store(accs):
        for h in range(heads):
            o_ref[:, h * HEAD_DIM:(h + 1) * HEAD_DIM] = accs[h].astype(o_ref.dtype)

    @pl.when(i == 0)
    def _():
        store([o for o, _ in blocks([(h, 0, True, no_suffix) for h in range(heads)])])

    @pl.when(i > 0)
    def _():
        res = blocks([(h, i, True, no_suffix) for h in range(heads)]
                     + [(h, i - 1, False, h) for h in range(heads)])
        accs = tuple(res[h][0] + res[heads + h][0] for h in range(heads))
        sufs = tuple(res[heads + h][1] for h in range(heads))

        def cond(carry):
            j, _, sufs = carry
            smallest = functools.reduce(jnp.minimum, sufs)
            return jnp.logical_and(j >= 0, jnp.min(smallest) < SB_ZERO_SUFFIX_BITS)

        def body(carry):
            j, accs, sufs = carry
            new = blocks([(h, j, False, sufs[h]) for h in range(heads)])
            return j - 1, tuple(accs[h] + new[h][0] for h in range(heads)), tuple(n[1] for n in new)

        _, accs, _ = lax.while_loop(cond, body, (i - 2, accs, sufs))
        store(accs)


def _stick_breaking(qkv, bsz, t, *, n_heads, blk=256, heads=4):
    nq = t // blk
    groups = n_heads // heads
    width = heads * HEAD_DIM
    ar = jnp.arange(blk)
    tri = (ar[:, None] >= ar[None, :]).astype(BF16)
    return pl.pallas_call(
        functools.partial(_sb_kernel, blk=blk, heads=heads, scale=HEAD_DIM ** -0.5),
        out_shape=jax.ShapeDtypeStruct((bsz * t, n_heads * HEAD_DIM), BF16),
        grid=(bsz, groups, nq),
        in_specs=[
            pl.BlockSpec((blk, width), lambda b, g, i: (b * nq + i, g)),
            pl.BlockSpec((t, width), lambda b, g, i: (b, groups + g)),
            pl.BlockSpec((t, width), lambda b, g, i: (b, 2 * groups + g)),
            pl.BlockSpec((blk, blk), lambda b, g, i: (0, 0)),
        ],
        out_specs=pl.BlockSpec((blk, width), lambda b, g, i: (b * nq + i, g)),
        compiler_params=_params("parallel", "parallel", "arbitrary"),
        name="stick_breaking",
    )(qkv, qkv, qkv, tri)


def kernel(x, norm_pre_mix, norm_post_mix, norm_pre_ffn, norm_post_ffn, ab_w_in, sgu_ln_g, sgu_ln_b, sgu_w, sgu_b,
           ab_w_out, sb_w_in, sb_w_out, ffn_w1, ffn_w2):
    bsz, t, d = x.shape
    a_width = sgu_ln_g.shape[-1]
    b_heads = (ab_w_in.shape[-1] - 2 * a_width) // (3 * HEAD_DIM)
    c_heads = sb_w_in.shape[-1] // (3 * HEAD_DIM)
    assert a_width == b_heads * HEAD_DIM, "dilated q/k/v column blocks are addressed in units of a_width"

    xs = x.reshape(bsz * t, d)
    row = lambda v: v.reshape(1, -1).astype(F32)

    assert tuple(dil for _, dil in DILATED_PAIRS) == (1, 4, 16), "proj_qkv0 regroups q/k/v for dilations 1, 4, 16"
    w_in = _to_bf16(ab_w_in, 0)
    uv = _proj_uv(xs, row(norm_pre_mix[0]), w_in, row(sgu_ln_g[0]), row(sgu_ln_b[0]))
    grouped = _proj_qkv0(xs, row(norm_pre_mix[0]), w_in, bsz, t, tn=a_width, first_block=2,
                         q_scale=HEAD_DIM ** -0.5 * LOG2_E)
    bias_full = jnp.repeat(sgu_b[0].T.astype(F32), CHUNK, axis=1)
    a_out = _sgu(uv, sgu_w[0], bias_full, a_width=a_width)
    outs, lses = zip(*[_dilated_branch(zd, t, window, dilation, n_heads=b_heads, qb=qb, rb=rb)
                       for zd, (window, dilation), (qb, rb) in zip(grouped, DILATED_PAIRS, DILATED_STEP)])
    lses = [lse.reshape(bsz * t, 128) for lse in lses]
    xs = _out_merge(a_out, outs, lses, _to_bf16(ab_w_out, 0), xs, row(norm_post_mix[0]))
    xs = _ffn(xs, row(norm_pre_ffn[0]), _to_bf16(ffn_w1, 0), _to_bf16(ffn_w2, 0), row(norm_post_ffn[0]))

    qkv = _norm_matmul(xs, row(norm_pre_mix[1]), _to_bf16(sb_w_in, 0))
    att = _stick_breaking(qkv, bsz, t, n_heads=c_heads)
    xs = _out_proj(att, _to_bf16(sb_w_out, 0), xs, row(norm_post_mix[1]))
    xs = _ffn(xs, row(norm_pre_ffn[1]), _to_bf16(ffn_w1, 1), _to_bf16(ffn_w2, 1), row(norm_post_ffn[1]))
    return xs.reshape(bsz, t, d)
```

```python
import functools
import math

import jax
import jax.numpy as jnp
from jax import lax
from jax.experimental import pallas as pl
from jax.experimental.pallas import tpu as pltpu

HEAD_DIM = 128
CHUNK = 128
ATT_BLOCK = 128
DILATED_PAIRS = ((128, 1), (512, 4), (2048, 16))
DILATED_STEP = ((8, 1), (2, 4), (1, 4))
RMS_EPS = 1e-6
LN_EPS = 1e-5
NEG_BIG = -1e30
LOG2_E = 1.4426950408889634

V7X_VMEM_LIMIT_BYTES = 56 * 1024 * 1024

BF16 = jnp.bfloat16
F32 = jnp.float32


def _params(*semantics):
    return pltpu.CompilerParams(dimension_semantics=semantics, vmem_limit_bytes=V7X_VMEM_LIMIT_BYTES)


def _const_spec(shape):
    return pl.BlockSpec(shape, lambda i: (0,) * len(shape), pipeline_mode=pl.Buffered(1))


def _cast_kernel(w_ref, o_ref):
    o_ref[...] = w_ref[...].astype(o_ref.dtype)


def _to_bf16(w, layer, *, block_bytes=8 * 1024 * 1024):
    _, r, c = w.shape
    br = r
    while 4 * br * c > block_bytes and br % 32 == 0:
        br //= 2
    return pl.pallas_call(
        _cast_kernel,
        out_shape=jax.ShapeDtypeStruct((r, c), BF16),
        grid=(r // br,),
        in_specs=[pl.BlockSpec((None, br, c), lambda i: (layer, i, 0))],
        out_specs=pl.BlockSpec((br, c), lambda i: (i, 0)),
        compiler_params=_params("parallel"),
        name="to_bf16",
    )(w)


def _rms_scale(x):
    return lax.rsqrt(jnp.mean(x * x, axis=-1, keepdims=True) + RMS_EPS)


def _gelu(x):
    return 0.5 * x * (1.0 + lax.erf(x * math.sqrt(0.5)))


def _normed_row_chunks(x_ref, g_ref, rc):
    for c in range(x_ref.shape[0] // rc):
        rows = slice(c * rc, (c + 1) * rc)
        x = x_ref[rows, :]
        yield c, rows, (x * _rms_scale(x) * g_ref[...]).astype(BF16)


def _norm_matmul_kernel(x_ref, g_ref, w_ref, o_ref, *, rc, tn):
    for _, rows, h in _normed_row_chunks(x_ref, g_ref, rc):
        for n in range(o_ref.shape[1] // tn):
            cols = slice(n * tn, (n + 1) * tn)
            o_ref[rows, cols] = jnp.dot(h, w_ref[:, cols], preferred_element_type=F32).astype(o_ref.dtype)


def _norm_matmul(x, g, w, *, tm=512, rc=256, tn=1024):
    m, d = x.shape
    n = w.shape[1]
    return pl.pallas_call(
        functools.partial(_norm_matmul_kernel, rc=rc, tn=tn),
        out_shape=jax.ShapeDtypeStruct((m, n), BF16),
        grid=(m // tm,),
        in_specs=[pl.BlockSpec((tm, d), lambda i: (i, 0)), _const_spec((1, d)), _const_spec((d, n))],
        out_specs=pl.BlockSpec((tm, n), lambda i: (i, 0)),
        compiler_params=_params("parallel"),
        name="norm_matmul",
    )(x, g, w)


def _proj_uv_kernel(x_ref, g_ref, w_ref, lng_ref, lnb_ref, uv_ref, *, rc):
    tn = lng_ref.shape[1]
    for _, rows, h in _normed_row_chunks(x_ref, g_ref, rc):
        v = _gelu(jnp.dot(h, w_ref[:, tn:], preferred_element_type=F32))
        mu = jnp.mean(v, axis=-1, keepdims=True)
        c = v - mu
        var = jnp.mean(c * c, axis=-1, keepdims=True)
        uv_ref[rows, tn:] = (c * lax.rsqrt(var + LN_EPS) * lng_ref[...] + lnb_ref[...]).astype(uv_ref.dtype)
        u = jnp.dot(h, w_ref[:, :tn], preferred_element_type=F32)
        uv_ref[rows, :tn] = _gelu(u).astype(uv_ref.dtype)


def _proj_uv(x, g, w, ln_g, ln_b, *, tm=512, rc=256):
    m, d = x.shape
    n = 2 * ln_g.shape[1]
    return pl.pallas_call(
        functools.partial(_proj_uv_kernel, rc=rc),
        out_shape=jax.ShapeDtypeStruct((m, n), BF16),
        grid=(m // tm,),
        in_specs=[pl.BlockSpec((tm, d), lambda i: (i, 0)), _const_spec((1, d)), _const_spec((d, n)),
                  _const_spec(ln_g.shape), _const_spec(ln_b.shape)],
        out_specs=pl.BlockSpec((tm, n), lambda i: (i, 0)),
        compiler_params=_params("parallel"),
        name="proj_uv",
    )(x, g, w, ln_g, ln_b)


def _proj_qkv0_kernel(x_ref, g_ref, wq_ref, wk_ref, wv_ref, z1_ref, z4_ref, z16_ref, s_ref, s4_ref, *, rc, q_scale):
    tn = wq_ref.shape[1]
    slabs = tn // 128
    q4, q16 = rc // 4, rc // 16
    for c, rows, h in _normed_row_chunks(x_ref, g_ref, rc):
        rows4 = slice(c * q4, (c + 1) * q4)
        rows16 = slice(c * q16, (c + 1) * q16)
        for part, w_ref in enumerate((wq_ref, wk_ref, wv_ref)):
            acc = jnp.dot(h, w_ref[...], preferred_element_type=F32)
            if part == 0:
                acc = acc * q_scale
            z1_ref[rows, part * tn:(part + 1) * tn] = acc.astype(z1_ref.dtype)
            for s in range(slabs):
                s_ref[s] = acc[:, s * 128:(s + 1) * 128]
            for r in range(4):
                for s in range(slabs):
                    col = (part * 4 + r) * tn + s * 128
                    piece = s_ref[s, pl.ds(r, q4, stride=4), :]
                    z4_ref[rows4, col:col + 128] = piece.astype(z4_ref.dtype)
                    s4_ref[s, r * q4:(r + 1) * q4, :] = piece
            for r in range(16):
                for s in range(slabs):
                    col = (part * 16 + r) * tn + s * 128
                    piece = s4_ref[s, pl.ds((r % 4) * q4 + r // 4, q16, stride=4), :]
                    z16_ref[rows16, col:col + 128] = piece.astype(z16_ref.dtype)


def _proj_qkv0(x, g, w, bsz, t, *, tn, first_block, q_scale, tm=512, rc=256):
    m, d = x.shape
    n = 3 * tn
    tiles = t // tm
    grouped = lambda dil: pl.BlockSpec((None, tm // dil, dil * n), lambda i: (i // tiles, i % tiles, 0))
    w_block = lambda part: pl.BlockSpec((d, tn), lambda i: (0, first_block + part), pipeline_mode=pl.Buffered(1))
    return pl.pallas_call(
        functools.partial(_proj_qkv0_kernel, rc=rc, q_scale=q_scale),
        out_shape=tuple(jax.ShapeDtypeStruct((bsz, t // dil, dil * n), BF16) for dil in (1, 4, 16)),
        grid=(m // tm,),
        in_specs=[pl.BlockSpec((tm, d), lambda i: (i, 0)), _const_spec((1, d)), w_block(0), w_block(1), w_block(2)],
        out_specs=(grouped(1), grouped(4), grouped(16)),
        scratch_shapes=[pltpu.VMEM((tn // 128, rc, 128), F32), pltpu.VMEM((tn // 128, rc, 128), F32)],
        compiler_params=_params("parallel"),
        name="proj_qkv0",
    )(x, g, w, w, w)


def _sgu_kernel(u_ref, v_ref, w_ref, b_ref, o_ref, *, chunks, groups):
    row = lax.broadcasted_iota(jnp.int32, (CHUNK, CHUNK), 0)
    col = lax.broadcasted_iota(jnp.int32, (CHUNK, CHUNK), 1)
    for g in range(groups):
        cs = slice(g * CHUNK, (g + 1) * CHUNK)
        wg = jnp.where(row >= col, w_ref[g], 0.0).astype(BF16)
        bias = b_ref[:, cs]
        for c in range(chunks):
            rs = slice(c * CHUNK, (c + 1) * CHUNK)
            mixed = jnp.dot(wg, v_ref[rs, cs], preferred_element_type=F32) + bias
            o_ref[rs, cs] = (u_ref[rs, cs].astype(F32) * mixed).astype(o_ref.dtype)


def _sgu(z, w_s, bias_full, *, a_width, chunks=16):
    m = z.shape[0]
    groups = w_s.shape[0]
    tm = chunks * CHUNK
    return pl.pallas_call(
        functools.partial(_sgu_kernel, chunks=chunks, groups=groups),
        out_shape=jax.ShapeDtypeStruct((m, a_width), BF16),
        grid=(m // tm,),
        in_specs=[
            pl.BlockSpec((tm, a_width), lambda i: (i, 0)),
            pl.BlockSpec((tm, a_width), lambda i: (i, 1)),
            pl.BlockSpec((groups, CHUNK, CHUNK), lambda i: (0, 0, 0)),
            pl.BlockSpec((CHUNK, a_width), lambda i: (0, 0)),
        ],
        out_specs=pl.BlockSpec((tm, a_width), lambda i: (i, 0)),
        compiler_params=_params("parallel"),
        name="sgu",
    )(z, z, w_s, bias_full)


def _dilated_kernel(q_ref, kp_ref, kc_ref, vp_ref, vc_ref, bias_ref, o_ref, lse_ref, *, n_heads, dilation, qb, rb):
    blk = ATT_BLOCK
    width = n_heads * HEAD_DIM
    at_start = pl.program_id(1) == 0
    lane = lax.broadcasted_iota(jnp.int32, (blk, 128), 1)
    ones = jnp.ones((2 * blk, HEAD_DIM), BF16)
    for u, rr in [(u, rr) for u in range(qb) for rr in range(rb)]:
        first = jnp.where(at_start, 0, 1) if u == 0 else 1
        start = u * blk * dilation + pl.program_id(2) * rb + rr
        rows = pl.ds(start, blk, stride=dilation) if dilation > 1 else slice(u * blk, (u + 1) * blk)
        cur = slice(u * blk, (u + 1) * blk)
        before = slice((u - 1) * blk, u * blk)
        lse = jnp.zeros((blk, 128), F32)
        for h in range(n_heads):
            sl = slice(rr * width + h * HEAD_DIM, rr * width + (h + 1) * HEAD_DIM)
            k_prev = kp_ref[:, sl] if u == 0 else kc_ref[before, sl]
            v_prev = vp_ref[:, sl] if u == 0 else vc_ref[before, sl]
            k = jnp.concatenate([k_prev, kc_ref[cur, sl]], axis=0)
            v = jnp.concatenate([v_prev, vc_ref[cur, sl]], axis=0)
            s = lax.dot_general(q_ref[cur, sl], k, (((1,), (1,)), ((), ())), preferred_element_type=F32)
            s = s + bias_ref[first, h]
            m = jnp.max(s, axis=-1, keepdims=True)
            p = jnp.exp2(s - m).astype(BF16)
            pv = jnp.dot(p, jnp.concatenate([v, ones], axis=1), preferred_element_type=F32)
            den = pv[:, HEAD_DIM:]
            o_ref[h, rows, :] = (pv[:, :HEAD_DIM] * (1.0 / den)).astype(o_ref.dtype)
            lse = jnp.where(lane == h, m + jnp.log(den) * LOG2_E, lse)
        lse_ref[rows, :] = lse


def _dilated_bias(window, dilation, n_heads):
    blk = ATT_BLOCK
    win_sub = window // dilation
    qi = jnp.arange(blk)[:, None]
    kj = jnp.arange(2 * blk)[None, :]
    dist = qi + blk - kj
    band = (dist >= 0) & (dist <= win_sub)
    slopes = jnp.exp2(-8.0 * (jnp.arange(n_heads, dtype=F32) + 1.0) / n_heads)
    bias = -slopes[:, None, None] * (dist * dilation).astype(F32)[None] * LOG2_E
    later = jnp.where(band[None], bias, NEG_BIG)
    first = jnp.where((band & (kj >= blk))[None], bias, NEG_BIG)
    return jnp.stack([first, later], axis=0)


def _dilated_branch(zd, t, window, dilation, *, n_heads, qb, rb):
    bsz = zd.shape[0]
    width = n_heads * HEAD_DIM
    d = dilation
    steps = t // (d * ATT_BLOCK * qb)
    span = qb * d * ATT_BLOCK
    groups = d // rb
    bias = _dilated_bias(window, d, n_heads)

    def cur(part):
        return pl.BlockSpec((None, qb * ATT_BLOCK, rb * width), lambda b, n, r: (b, n, part * groups + r))

    def prev(part):
        return pl.BlockSpec((None, ATT_BLOCK, rb * width),
                            lambda b, n, r: (b, jnp.maximum(qb * n - 1, 0), part * groups + r))

    return pl.pallas_call(
        functools.partial(_dilated_kernel, n_heads=n_heads, dilation=d, qb=qb, rb=rb),
        out_shape=(jax.ShapeDtypeStruct((bsz, n_heads, t, HEAD_DIM), BF16 if d == 1 else F32),
                   jax.ShapeDtypeStruct((bsz, t, 128), F32)),
        grid=(bsz, steps, groups),
        in_specs=[cur(0), prev(1), cur(1), prev(2), cur(2),
                  pl.BlockSpec((2, n_heads, ATT_BLOCK, 2 * ATT_BLOCK), lambda b, n, r: (0, 0, 0, 0))],
        out_specs=(pl.BlockSpec((None, n_heads, span, HEAD_DIM), lambda b, n, r: (b, 0, n, 0)),
                   pl.BlockSpec((None, span, 128), lambda b, n, r: (b, n, 0))),
        compiler_params=_params("parallel", "arbitrary", "arbitrary"),
        name=f"dilated_d{d}",
    )(zd, zd, zd, zd, zd, bias)


def _post_norm_residual(x_ref, y, g_ref, o_ref):
    o_ref[...] = x_ref[...] + y * _rms_scale(y) * g_ref[...]


def _out_merge_kernel(a_ref, o1_ref, o2_ref, o3_ref, l1_ref, l2_ref, l3_ref, w_ref, x_ref, g_ref, out_ref,
                      *, n_heads, a_width, rc):
    for c in range(x_ref.shape[0] // rc):
        rows = slice(c * rc, (c + 1) * rc)
        l1, l2, l3 = l1_ref[rows, :], l2_ref[rows, :], l3_ref[rows, :]
        m = jnp.maximum(jnp.maximum(l1, l2), l3)
        e1, e2, e3 = jnp.exp2(l1 - m), jnp.exp2(l2 - m), jnp.exp2(l3 - m)
        inv = 1.0 / (e1 + e2 + e3)
        w1, w2, w3 = e1 * inv, e2 * inv, e3 * inv
        parts = []
        for h in range(n_heads):
            bh = (w1[:, h:h + 1] * o1_ref[h, rows, :] + w2[:, h:h + 1] * o2_ref[h, rows, :]
                  + w3[:, h:h + 1] * o3_ref[h, rows, :])
            parts.append(bh.astype(BF16))
        b_out = jnp.concatenate(parts, axis=1)
        y = (jnp.dot(a_ref[rows, :], w_ref[:a_width, :], preferred_element_type=F32)
             + jnp.dot(b_out, w_ref[a_width:, :], preferred_element_type=F32))
        out_ref[rows, :] = x_ref[rows, :] + y * _rms_scale(y) * g_ref[...]


def _out_merge(a_out, outs, lses, w_out, x, g, *, tm=512, rc=256):
    m, d = x.shape
    a_width = a_out.shape[1]
    bsz, n_heads, t, _ = outs[0].shape
    tiles = t // tm
    row = lambda width: pl.BlockSpec((tm, width), lambda i: (i, 0))
    heads = pl.BlockSpec((None, n_heads, tm, HEAD_DIM), lambda i: (i // tiles, 0, i % tiles, 0))
    return pl.pallas_call(
        functools.partial(_out_merge_kernel, n_heads=n_heads, a_width=a_width, rc=rc),
        out_shape=jax.ShapeDtypeStruct((m, d), F32),
        grid=(m // tm,),
        in_specs=[row(a_width), heads, heads, heads, row(128), row(128), row(128),
                  _const_spec(w_out.shape), row(d), _const_spec((1, d))],
        out_specs=row(d),
        compiler_params=_params("parallel"),
        name="out_merge",
    )(a_out, *outs, *lses, w_out, x, g)


def _out_proj_kernel(a_ref, w_ref, x_ref, g_ref, out_ref, *, rc):
    for c in range(x_ref.shape[0] // rc):
        rows = slice(c * rc, (c + 1) * rc)
        y = jnp.dot(a_ref[rows, :], w_ref[...], preferred_element_type=F32)
        out_ref[rows, :] = x_ref[rows, :] + y * _rms_scale(y) * g_ref[...]


def _out_proj(a, w_out, x, g, *, tm=512, rc=256):
    m, d = x.shape
    row = lambda width: pl.BlockSpec((tm, width), lambda i: (i, 0))
    return pl.pallas_call(
        functools.partial(_out_proj_kernel, rc=rc),
        out_shape=jax.ShapeDtypeStruct((m, d), F32),
        grid=(m // tm,),
        in_specs=[row(a.shape[1]), _const_spec(w_out.shape), row(d), _const_spec((1, d))],
        out_specs=row(d),
        compiler_params=_params("parallel"),
        name="out_proj",
    )(a, w_out, x, g)


def _ffn_kernel(x_ref, g1_ref, w1_ref, w2_ref, g2_ref, o_ref, h_ref, acc_ref, *, rc):
    f = pl.program_id(1)
    last = pl.num_programs(1) - 1
    chunks = [slice(c * rc, (c + 1) * rc) for c in range(x_ref.shape[0] // rc)]

    def mlp(h):
        t = jnp.maximum(jnp.dot(h, w1_ref[...], preferred_element_type=F32), 0.0)
        return jnp.dot((t * t).astype(BF16), w2_ref[...], preferred_element_type=F32)

    @pl.when(f == 0)
    def _():
        for rows in chunks:
            x = x_ref[rows, :]
            h = (x * _rms_scale(x) * g1_ref[...]).astype(h_ref.dtype)
            h_ref[rows, :] = h
            acc_ref[rows, :] = mlp(h)

    @pl.when(jnp.logical_and(f > 0, f < last))
    def _():
        acc_ref[...] += mlp(h_ref[...])

    @pl.when(f == last)
    def _():
        for rows in chunks:
            y = acc_ref[rows, :] + mlp(h_ref[rows, :])
            o_ref[rows, :] = x_ref[rows, :] + y * _rms_scale(y) * g2_ref[...]


def _ffn(x, g1, w1, w2, g2, *, tm=512, rc=256, tf=1024):
    m, d = x.shape
    d_ff = w1.shape[1]
    assert d_ff // tf >= 2, "the first and the last hidden slice are distinct grid steps"
    return pl.pallas_call(
        functools.partial(_ffn_kernel, rc=rc),
        out_shape=jax.ShapeDtypeStruct((m, d), F32),
        grid=(m // tm, d_ff // tf),
        in_specs=[
            pl.BlockSpec((tm, d), lambda i, f: (i, 0)),
            pl.BlockSpec((1, d), lambda i, f: (0, 0)),
            pl.BlockSpec((d, tf), lambda i, f: (0, f)),
            pl.BlockSpec((tf, d), lambda i, f: (f, 0)),
            pl.BlockSpec((1, d), lambda i, f: (0, 0)),
        ],
        out_specs=pl.BlockSpec((tm, d), lambda i, f: (i, 0)),
        scratch_shapes=[pltpu.VMEM((tm, d), BF16), pltpu.VMEM((tm, d), F32)],
        compiler_params=_params("parallel", "arbitrary"),
        name="ffn",
    )(x, g1, w1, w2, g2)


SB_ZERO_SUFFIX_BITS = 152.0


def _sb_kernel(q_ref, k_ref, v_ref, tri_ref, o_ref, *, blk, heads, scale):
    i = pl.program_id(2)
    tri = tri_ref[...]
    row = lax.broadcasted_iota(jnp.int32, (blk, blk), 0)
    col = lax.broadcasted_iota(jnp.int32, (blk, blk), 1)
    causal = col < row
    no_suffix = jnp.zeros((blk, 1), F32)

    def blocks(jobs):
        cols = [slice(h * HEAD_DIM, (h + 1) * HEAD_DIM) for h, _, _, _ in jobs]
        starts = [pl.multiple_of(j * blk, blk) for _, j, _, _ in jobs]
        z2s = [lax.dot_general(q_ref[:, sl], k_ref[pl.ds(st, blk), sl], (((1,), (1,)), ((), ())),
                               preferred_element_type=F32) * (scale * LOG2_E) for sl, st in zip(cols, starts)]
        sps = [jnp.maximum(z2, 0.0) + jnp.log(1.0 + jnp.exp2(-jnp.abs(z2))) * LOG2_E for z2 in z2s]
        sps = [jnp.where(causal, sp, 0.0) if job[2] else sp for sp, job in zip(sps, jobs)]
        sums = [jnp.dot(sp.astype(BF16), tri, preferred_element_type=F32) for sp in sps]
        cums = []
        for part, (_, _, _, suffix) in zip(sums, jobs):
            cums.append(part + (cums[suffix][:, 0:1] if isinstance(suffix, int) else suffix))
        weights = [jnp.exp2(z2 - cum) for z2, cum in zip(z2s, cums)]
        weights = [jnp.where(causal, a, 0.0) if job[2] else a for a, job in zip(weights, jobs)]
        outs = [jnp.dot(a.astype(BF16), v_ref[pl.ds(st, blk), sl], preferred_element_type=F32)
                for a, sl, st in zip(weights, cols, starts)]
        return [(o, cum[:, 0:1]) for o, cum in zip(outs, cums)]

    def store(accs):
        for h in range(heads):
            o_ref[:, h * HEAD_DIM:(h + 1) * HEAD_DIM] = accs[h].astype(o_ref.dtype)

    @pl.when(i == 0)
    def _():
        store([o for o, _ in blocks([(h, 0, True, no_suffix) for h in range(heads)])])

    @pl.when(i > 0)
    def _():
        res = blocks([(h, i, True, no_suffix) for h in range(heads)]
                     + [(h, i - 1, False, h) for h in range(heads)])
        accs = tuple(res[h][0] + res[heads + h][0] for h in range(heads))
        sufs = tuple(res[heads + h][1] for h in range(heads))

        def cond(carry):
            j, _, sufs = carry
            smallest = functools.reduce(jnp.minimum, sufs)
            return jnp.logical_and(j >= 0, jnp.min(smallest) < SB_ZERO_SUFFIX_BITS)

        def body(carry):
            j, accs, sufs = carry
            new = blocks([(h, j, False, sufs[h]) for h in range(heads)])
            return j - 1, tuple(accs[h] + new[h][0] for h in range(heads)), tuple(n[1] for n in new)

        _, accs, _ = lax.while_loop(cond, body, (i - 2, accs, sufs))
        store(accs)


def _stick_breaking(qkv, bsz, t, *, n_heads, blk=256, heads=4):
    nq = t // blk
    groups = n_heads // heads
    width = heads * HEAD_DIM
    ar = jnp.arange(blk)
    tri = (ar[:, None] >= ar[None, :]).astype(BF16)
    return pl.pallas_call(
        functools.partial(_sb_kernel, blk=blk, heads=heads, scale=HEAD_DIM ** -0.5),
        out_shape=jax.ShapeDtypeStruct((bsz * t, n_heads * HEAD_DIM), BF16),
        grid=(bsz, groups, nq),
        in_specs=[
            pl.BlockSpec((blk, width), lambda b, g, i: (b * nq + i, g)),
            pl.BlockSpec((t, width), lambda b, g, i: (b, groups + g)),
            pl.BlockSpec((t, width), lambda b, g, i: (b, 2 * groups + g)),
            pl.BlockSpec((blk, blk), lambda b, g, i: (0, 0)),
        ],
        out_specs=pl.BlockSpec((blk, width), lambda b, g, i: (b * nq + i, g)),
        compiler_params=_params("parallel", "parallel", "arbitrary"),
        name="stick_breaking",
    )(qkv, qkv, qkv, tri)


def kernel(x, norm_pre_mix, norm_post_mix, norm_pre_ffn, norm_post_ffn, ab_w_in, sgu_ln_g, sgu_ln_b, sgu_w, sgu_b,
           ab_w_out, sb_w_in, sb_w_out, ffn_w1, ffn_w2):
    bsz, t, d = x.shape
    a_width = sgu_ln_g.shape[-1]
    b_heads = (ab_w_in.shape[-1] - 2 * a_width) // (3 * HEAD_DIM)
    c_heads = sb_w_in.shape[-1] // (3 * HEAD_DIM)
    assert a_width == b_heads * HEAD_DIM, "dilated q/k/v column blocks are addressed in units of a_width"

    xs = x.reshape(bsz * t, d)
    row = lambda v: v.reshape(1, -1).astype(F32)

    assert tuple(dil for _, dil in DILATED_PAIRS) == (1, 4, 16), "proj_qkv0 regroups q/k/v for dilations 1, 4, 16"
    w_in = _to_bf16(ab_w_in, 0)
    uv = _proj_uv(xs, row(norm_pre_mix[0]), w_in, row(sgu_ln_g[0]), row(sgu_ln_b[0]))
    grouped = _proj_qkv0(xs, row(norm_pre_mix[0]), w_in, bsz, t, tn=a_width, first_block=2,
                         q_scale=HEAD_DIM ** -0.5 * LOG2_E)
    bias_full = jnp.repeat(sgu_b[0].T.astype(F32), CHUNK, axis=1)
    a_out = _sgu(uv, sgu_w[0], bias_full, a_width=a_width)
    outs, lses = zip(*[_dilated_branch(zd, t, window, dilation, n_heads=b_heads, qb=qb, rb=rb)
                       for zd, (window, dilation), (qb, rb) in zip(grouped, DILATED_PAIRS, DILATED_STEP)])
    lses = [lse.reshape(bsz * t, 128) for lse in lses]
    xs = _out_merge(a_out, outs, lses, _to_bf16(ab_w_out, 0), xs, row(norm_post_mix[0]))
    xs = _ffn(xs, row(norm_pre_ffn[0]), _to_bf16(ffn_w1, 0), _to_bf16(ffn_w2, 0), row(norm_post_ffn[0]))

    qkv = _norm_matmul(xs, row(norm_pre_mix[1]), _to_bf16(sb_w_in, 0))
    att = _stick_breaking(qkv, bsz, t, n_heads=c_heads)
    xs = _out_proj(att, _to_bf16(sb_w_out, 0), xs, row(norm_post_mix[1]))
    xs = _ffn(xs, row(norm_pre_ffn[1]), _to_bf16(ffn_w1, 1), _to_bf16(ffn_w2, 1), row(norm_post_ffn[1]))
    return xs.reshape(bsz, t, d)
```

```python
import functools
import math

import jax
import jax.numpy as jnp
from jax import lax
from jax.experimental import pallas as pl
from jax.experimental.pallas import tpu as pltpu

HEAD_DIM = 128
CHUNK = 128
ATT_BLOCK = 128
DILATED_PAIRS = ((128, 1), (512, 4), (2048, 16))
DILATED_STEP = ((8, 1), (2, 4), (1, 4))
RMS_EPS = 1e-6
LN_EPS = 1e-5
NEG_BIG = -1e30
LOG2_E = 1.4426950408889634

V7X_VMEM_LIMIT_BYTES = 56 * 1024 * 1024

BF16 = jnp.bfloat16
F32 = jnp.float32


def _params(*semantics):
    return pltpu.CompilerParams(dimension_semantics=semantics, vmem_limit_bytes=V7X_VMEM_LIMIT_BYTES)


def _const_spec(shape):
    return pl.BlockSpec(shape, lambda i: (0,) * len(shape), pipeline_mode=pl.Buffered(1))


def _cast_kernel(w_ref, o_ref):
    o_ref[...] = w_ref[...].astype(o_ref.dtype)


def _to_bf16(w, layer, *, block_bytes=8 * 1024 * 1024):
    _, r, c = w.shape
    br = r
    while 4 * br * c > block_bytes and br % 32 == 0:
        br //= 2
    return pl.pallas_call(
        _cast_kernel,
        out_shape=jax.ShapeDtypeStruct((r, c), BF16),
        grid=(r // br,),
        in_specs=[pl.BlockSpec((None, br, c), lambda i: (layer, i, 0))],
        out_specs=pl.BlockSpec((br, c), lambda i: (i, 0)),
        compiler_params=_params("parallel"),
        name="to_bf16",
    )(w)


def _side_cast_plan(weights, n_steps, step_of):
    in_specs, out_specs, out_shapes, n_blocks = [], [], [], []
    for w, layer in weights:
        _, r, c = w.shape
        nb = n_steps
        while nb > 1 and (r % nb or (r // nb) % 16):
            nb //= 2
        assert r % nb == 0 and (r // nb) % 16 == 0, (r, n_steps)
        rb = r // nb
        block = lambda *ids, nb=nb: jnp.minimum(step_of(*ids), nb - 1)
        in_specs.append(pl.BlockSpec((None, rb, c), lambda *ids, layer=layer, block=block: (layer, block(*ids), 0)))
        out_specs.append(pl.BlockSpec((rb, c), lambda *ids, block=block: (block(*ids), 0)))
        out_shapes.append(jax.ShapeDtypeStruct((r, c), BF16))
        n_blocks.append(nb)
    return in_specs, out_specs, out_shapes, n_blocks


def _side_cast(step, w_refs, o_refs, n_blocks):
    for w_ref, o_ref, nb in zip(w_refs, o_refs, n_blocks):
        @pl.when(step < nb)
        def _(w_ref=w_ref, o_ref=o_ref):
            o_ref[...] = w_ref[...].astype(o_ref.dtype)


def _rms_scale(x):
    return lax.rsqrt(jnp.mean(x * x, axis=-1, keepdims=True) + RMS_EPS)


def _gelu(x):
    return 0.5 * x * (1.0 + lax.erf(x * math.sqrt(0.5)))


def _normed_row_chunks(x_ref, g_ref, rc):
    for c in range(x_ref.shape[0] // rc):
        rows = slice(c * rc, (c + 1) * rc)
        x = x_ref[rows, :]
        yield c, rows, (x * _rms_scale(x) * g_ref[...]).astype(BF16)


def _norm_matmul_kernel(x_ref, g_ref, w_ref, o_ref, *, rc, tn, q_cols, q_scale):
    for _, rows, h in _normed_row_chunks(x_ref, g_ref, rc):
        for n in range(o_ref.shape[1] // tn):
            cols = slice(n * tn, (n + 1) * tn)
            acc = jnp.dot(h, w_ref[:, cols], preferred_element_type=F32)
            if (n + 1) * tn <= q_cols:
                acc = acc * q_scale
            o_ref[rows, cols] = acc.astype(o_ref.dtype)


def _norm_matmul(x, g, w, *, q_cols, q_scale, tm=512, rc=256, tn=1024):
    m, d = x.shape
    n = w.shape[1]
    assert q_cols % tn == 0
    return pl.pallas_call(
        functools.partial(_norm_matmul_kernel, rc=rc, tn=tn, q_cols=q_cols, q_scale=q_scale),
        out_shape=jax.ShapeDtypeStruct((m, n), BF16),
        grid=(m // tm,),
        in_specs=[pl.BlockSpec((tm, d), lambda i: (i, 0)), _const_spec((1, d)), _const_spec((d, n))],
        out_specs=pl.BlockSpec((tm, n), lambda i: (i, 0)),
        compiler_params=_params("parallel"),
        name="norm_matmul",
    )(x, g, w)


def _proj_uv_kernel(*refs, rc, cast_blocks):
    n_cast = len(cast_blocks)
    x_ref, g_ref, w_ref, lng_ref, lnb_ref = refs[:5]
    uv_ref = refs[5 + n_cast]
    _side_cast(pl.program_id(0), refs[5:5 + n_cast], refs[6 + n_cast:], cast_blocks)
    tn = lng_ref.shape[1]
    for _, rows, h in _normed_row_chunks(x_ref, g_ref, rc):
        v = _gelu(jnp.dot(h, w_ref[:, tn:], preferred_element_type=F32))
        mu = jnp.mean(v, axis=-1, keepdims=True)
        c = v - mu
        var = jnp.mean(c * c, axis=-1, keepdims=True)
        uv_ref[rows, tn:] = (c * lax.rsqrt(var + LN_EPS) * lng_ref[...] + lnb_ref[...]).astype(uv_ref.dtype)
        u = jnp.dot(h, w_ref[:, :tn], preferred_element_type=F32)
        uv_ref[rows, :tn] = _gelu(u).astype(uv_ref.dtype)


def _proj_uv(x, g, w, ln_g, ln_b, *, cast=(), tm=512, rc=256):
    m, d = x.shape
    n = 2 * ln_g.shape[1]
    cast_in, cast_out, cast_shapes, cast_blocks = _side_cast_plan(cast, m // tm, lambda i: i)
    return pl.pallas_call(
        functools.partial(_proj_uv_kernel, rc=rc, cast_blocks=tuple(cast_blocks)),
        out_shape=(jax.ShapeDtypeStruct((m, n), BF16), *cast_shapes),
        grid=(m // tm,),
        in_specs=[pl.BlockSpec((tm, d), lambda i: (i, 0)), _const_spec((1, d)), _const_spec((d, n)),
                  _const_spec(ln_g.shape), _const_spec(ln_b.shape), *cast_in],
        out_specs=(pl.BlockSpec((tm, n), lambda i: (i, 0)), *cast_out),
        compiler_params=_params("arbitrary"),
        name="proj_uv",
    )(x, g, w, ln_g, ln_b, *[w_ for w_, _ in cast])


def _proj_qkv0_kernel(*refs, rc, q_scale, cast_blocks):
    n_cast = len(cast_blocks)
    x_ref, g_ref, wq_ref, wk_ref, wv_ref = refs[:5]
    z1_ref, z4_ref, z16_ref = refs[5 + n_cast:8 + n_cast]
    s_ref, s4_ref = refs[8 + 2 * n_cast:]
    _side_cast(pl.program_id(0), refs[5:5 + n_cast], refs[8 + n_cast:8 + 2 * n_cast], cast_blocks)
    tn = wq_ref.shape[1]
    slabs = tn // 128
    q4, q16 = rc // 4, rc // 16
    for c, rows, h in _normed_row_chunks(x_ref, g_ref, rc):
        rows4 = slice(c * q4, (c + 1) * q4)
        rows16 = slice(c * q16, (c + 1) * q16)
        for part, w_ref in enumerate((wq_ref, wk_ref, wv_ref)):
            acc = jnp.dot(h, w_ref[...], preferred_element_type=F32)
            if part == 0:
                acc = acc * q_scale
            z1_ref[rows, part * tn:(part + 1) * tn] = acc.astype(z1_ref.dtype)
            for s in range(slabs):
                s_ref[s] = acc[:, s * 128:(s + 1) * 128]
            for r in range(4):
                for s in range(slabs):
                    col = (part * 4 + r) * tn + s * 128
                    piece = s_ref[s, pl.ds(r, q4, stride=4), :]
                    z4_ref[rows4, col:col + 128] = piece.astype(z4_ref.dtype)
                    s4_ref[s, r * q4:(r + 1) * q4, :] = piece
            for r in range(16):
                for s in range(slabs):
                    col = (part * 16 + r) * tn + s * 128
                    piece = s4_ref[s, pl.ds((r % 4) * q4 + r // 4, q16, stride=4), :]
                    z16_ref[rows16, col:col + 128] = piece.astype(z16_ref.dtype)


def _proj_qkv0(x, g, w, bsz, t, *, tn, first_block, q_scale, cast=(), tm=512, rc=256):
    m, d = x.shape
    n = 3 * tn
    tiles = t // tm
    grouped = lambda dil: pl.BlockSpec((None, tm // dil, dil * n), lambda i: (i // tiles, i % tiles, 0))
    w_block = lambda part: pl.BlockSpec((d, tn), lambda i: (0, first_block + part), pipeline_mode=pl.Buffered(1))
    cast_in, cast_out, cast_shapes, cast_blocks = _side_cast_plan(cast, m // tm, lambda i: i)
    return pl.pallas_call(
        functools.partial(_proj_qkv0_kernel, rc=rc, q_scale=q_scale, cast_blocks=tuple(cast_blocks)),
        out_shape=(*[jax.ShapeDtypeStruct((bsz, t // dil, dil * n), BF16) for dil in (1, 4, 16)], *cast_shapes),
        grid=(m // tm,),
        in_specs=[pl.BlockSpec((tm, d), lambda i: (i, 0)), _const_spec((1, d)), w_block(0), w_block(1), w_block(2),
                  *cast_in],
        out_specs=(grouped(1), grouped(4), grouped(16), *cast_out),
        scratch_shapes=[pltpu.VMEM((tn // 128, rc, 128), F32), pltpu.VMEM((tn // 128, rc, 128), F32)],
        compiler_params=_params("arbitrary"),
        name="proj_qkv0",
    )(x, g, w, w, w, *[w_ for w_, _ in cast])


def _sgu_kernel(u_ref, v_ref, w_ref, b_ref, o_ref, *, chunks, groups):
    row = lax.broadcasted_iota(jnp.int32, (CHUNK, CHUNK), 0)
    col = lax.broadcasted_iota(jnp.int32, (CHUNK, CHUNK), 1)
    for g in range(groups):
        cs = slice(g * CHUNK, (g + 1) * CHUNK)
        wg = jnp.where(row >= col, w_ref[g], 0.0).astype(BF16)
        bias = b_ref[:, cs]
        for c in range(chunks):
            rs = slice(c * CHUNK, (c + 1) * CHUNK)
            mixed = jnp.dot(wg, v_ref[rs, cs], preferred_element_type=F32) + bias
            o_ref[rs, cs] = (u_ref[rs, cs].astype(F32) * mixed).astype(o_ref.dtype)


def _sgu(z, w_s, bias_full, *, a_width, chunks=16):
    m = z.shape[0]
    groups = w_s.shape[0]
    tm = chunks * CHUNK
    return pl.pallas_call(
        functools.partial(_sgu_kernel, chunks=chunks, groups=groups),
        out_shape=jax.ShapeDtypeStruct((m, a_width), BF16),
        grid=(m // tm,),
        in_specs=[
            pl.BlockSpec((tm, a_width), lambda i: (i, 0)),
            pl.BlockSpec((tm, a_width), lambda i: (i, 1)),
            pl.BlockSpec((groups, CHUNK, CHUNK), lambda i: (0, 0, 0)),
            pl.BlockSpec((CHUNK, a_width), lambda i: (0, 0)),
        ],
        out_specs=pl.BlockSpec((tm, a_width), lambda i: (i, 0)),
        compiler_params=_params("parallel"),
        name="sgu",
    )(z, z, w_s, bias_full)


def _dilated_kernel(q_ref, kp_ref, kc_ref, vp_ref, vc_ref, bias_ref, o_ref, lse_ref, *, n_heads, dilation, qb, rb):
    blk = ATT_BLOCK
    width = n_heads * HEAD_DIM
    at_start = pl.program_id(1) == 0
    lane = lax.broadcasted_iota(jnp.int32, (blk, 128), 1)
    ones = jnp.ones((2 * blk, HEAD_DIM), BF16)
    for u, rr in [(u, rr) for u in range(qb) for rr in range(rb)]:
        first = jnp.where(at_start, 0, 1) if u == 0 else 1
        start = u * blk * dilation + pl.program_id(2) * rb + rr
        rows = pl.ds(start, blk, stride=dilation) if dilation > 1 else slice(u * blk, (u + 1) * blk)
        cur = slice(u * blk, (u + 1) * blk)
        before = slice((u - 1) * blk, u * blk)
        lse = jnp.zeros((blk, 128), F32)
        for h in range(n_heads):
            sl = slice(rr * width + h * HEAD_DIM, rr * width + (h + 1) * HEAD_DIM)
            k_prev = kp_ref[:, sl] if u == 0 else kc_ref[before, sl]
            v_prev = vp_ref[:, sl] if u == 0 else vc_ref[before, sl]
            k = jnp.concatenate([k_prev, kc_ref[cur, sl]], axis=0)
            v = jnp.concatenate([v_prev, vc_ref[cur, sl]], axis=0)
            s = lax.dot_general(q_ref[cur, sl], k, (((1,), (1,)), ((), ())), preferred_element_type=F32)
            s = s + bias_ref[first, h]
            m = jnp.max(s, axis=-1, keepdims=True)
            p = jnp.exp2(s - m).astype(BF16)
            pv = jnp.dot(p, jnp.concatenate([v, ones], axis=1), preferred_element_type=F32)
            den = pv[:, HEAD_DIM:]
            o_ref[h, rows, :] = (pv[:, :HEAD_DIM] * (1.0 / den)).astype(o_ref.dtype)
            lse = jnp.where(lane == h, m + jnp.log(den) * LOG2_E, lse)
        lse_ref[rows, :] = lse


def _dilated_bias(window, dilation, n_heads):
    blk = ATT_BLOCK
    win_sub = window // dilation
    qi = jnp.arange(blk)[:, None]
    kj = jnp.arange(2 * blk)[None, :]
    dist = qi + blk - kj
    band = (dist >= 0) & (dist <= win_sub)
    slopes = jnp.exp2(-8.0 * (jnp.arange(n_heads, dtype=F32) + 1.0) / n_heads)
    bias = -slopes[:, None, None] * (dist * dilation).astype(F32)[None] * LOG2_E
    later = jnp.where(band[None], bias, NEG_BIG)
    first = jnp.where((band & (kj >= blk))[None], bias, NEG_BIG)
    return jnp.stack([first, later], axis=0)


def _dilated_branch(zd, t, window, dilation, *, n_heads, qb, rb):
    bsz = zd.shape[0]
    width = n_heads * HEAD_DIM
    d = dilation
    steps = t // (d * ATT_BLOCK * qb)
    span = qb * d * ATT_BLOCK
    groups = d // rb
    bias = _dilated_bias(window, d, n_heads)

    def cur(part):
        return pl.BlockSpec((None, qb * ATT_BLOCK, rb * width), lambda b, n, r: (b, n, part * groups + r))

    def prev(part):
        return pl.BlockSpec((None, ATT_BLOCK, rb * width),
                            lambda b, n, r: (b, jnp.maximum(qb * n - 1, 0), part * groups + r))

    return pl.pallas_call(
        functools.partial(_dilated_kernel, n_heads=n_heads, dilation=d, qb=qb, rb=rb),
        out_shape=(jax.ShapeDtypeStruct((bsz, n_heads, t, HEAD_DIM), BF16 if d == 1 else F32),
                   jax.ShapeDtypeStruct((bsz, t, 128), F32)),
        grid=(bsz, steps, groups),
        in_specs=[cur(0), prev(1), cur(1), prev(2), cur(2),
                  pl.BlockSpec((2, n_heads, ATT_BLOCK, 2 * ATT_BLOCK), lambda b, n, r: (0, 0, 0, 0))],
        out_specs=(pl.BlockSpec((None, n_heads, span, HEAD_DIM), lambda b, n, r: (b, 0, n, 0)),
                   pl.BlockSpec((None, span, 128), lambda b, n, r: (b, n, 0))),
        compiler_params=_params("parallel", "arbitrary", "arbitrary"),
        name=f"dilated_d{d}",
    )(zd, zd, zd, zd, zd, bias)


def _post_norm_residual(x_ref, y, g_ref, o_ref):
    o_ref[...] = x_ref[...] + y * _rms_scale(y) * g_ref[...]


def _out_merge_kernel(a_ref, o1_ref, o2_ref, o3_ref, l1_ref, l2_ref, l3_ref, w_ref, x_ref, g_ref, out_ref,
                      *, n_heads, a_width, rc):
    for c in range(x_ref.shape[0] // rc):
        rows = slice(c * rc, (c + 1) * rc)
        l1, l2, l3 = l1_ref[rows, :], l2_ref[rows, :], l3_ref[rows, :]
        m = jnp.maximum(jnp.maximum(l1, l2), l3)
        e1, e2, e3 = jnp.exp2(l1 - m), jnp.exp2(l2 - m), jnp.exp2(l3 - m)
        inv = 1.0 / (e1 + e2 + e3)
        w1, w2, w3 = e1 * inv, e2 * inv, e3 * inv
        parts = []
        for h in range(n_heads):
            bh = (w1[:, h:h + 1] * o1_ref[h, rows, :] + w2[:, h:h + 1] * o2_ref[h, rows, :]
                  + w3[:, h:h + 1] * o3_ref[h, rows, :])
            parts.append(bh.astype(BF16))
        b_out = jnp.concatenate(parts, axis=1)
        y = (jnp.dot(a_ref[rows, :], w_ref[:a_width, :], preferred_element_type=F32)
             + jnp.dot(b_out, w_ref[a_width:, :], preferred_element_type=F32))
        out_ref[rows, :] = x_ref[rows, :] + y * _rms_scale(y) * g_ref[...]


def _out_merge(a_out, outs, lses, w_out, x, g, *, tm=512, rc=256):
    m, d = x.shape
    a_width = a_out.shape[1]
    bsz, n_heads, t, _ = outs[0].shape
    tiles = t // tm
    row = lambda width: pl.BlockSpec((tm, width), lambda i: (i, 0))
    heads = pl.BlockSpec((None, n_heads, tm, HEAD_DIM), lambda i: (i // tiles, 0, i % tiles, 0))
    return pl.pallas_call(
        functools.partial(_out_merge_kernel, n_heads=n_heads, a_width=a_width, rc=rc),
        out_shape=jax.ShapeDtypeStruct((m, d), F32),
        grid=(m // tm,),
        in_specs=[row(a_width), heads, heads, heads, row(128), row(128), row(128),
                  _const_spec(w_out.shape), row(d), _const_spec((1, d))],
        out_specs=row(d),
        compiler_params=_params("parallel"),
        name="out_merge",
    )(a_out, *outs, *lses, w_out, x, g)


def _out_proj_kernel(a_ref, w_ref, x_ref, g_ref, out_ref, *, rc):
    for c in range(x_ref.shape[0] // rc):
        rows = slice(c * rc, (c + 1) * rc)
        y = jnp.dot(a_ref[rows, :], w_ref[...], preferred_element_type=F32)
        out_ref[rows, :] = x_ref[rows, :] + y * _rms_scale(y) * g_ref[...]


def _out_proj(a, w_out, x, g, *, tm=512, rc=256):
    m, d = x.shape
    row = lambda width: pl.BlockSpec((tm, width), lambda i: (i, 0))
    return pl.pallas_call(
        functools.partial(_out_proj_kernel, rc=rc),
        out_shape=jax.ShapeDtypeStruct((m, d), F32),
        grid=(m // tm,),
        in_specs=[row(a.shape[1]), _const_spec(w_out.shape), row(d), _const_spec((1, d))],
        out_specs=row(d),
        compiler_params=_params("parallel"),
        name="out_proj",
    )(a, w_out, x, g)


def _ffn_kernel(*refs, rc, cast_blocks):
    n_cast = len(cast_blocks)
    x_ref, g1_ref, w1_ref, w2_ref, g2_ref = refs[:5]
    o_ref = refs[5 + n_cast]
    h_ref, acc_ref = refs[6 + 2 * n_cast:]
    f = pl.program_id(1)
    last = pl.num_programs(1) - 1
    _side_cast(pl.program_id(0) * pl.num_programs(1) + f, refs[5:5 + n_cast], refs[6 + n_cast:6 + 2 * n_cast],
               cast_blocks)
    chunks = [slice(c * rc, (c + 1) * rc) for c in range(x_ref.shape[0] // rc)]

    def mlp(h):
        t = jnp.maximum(jnp.dot(h, w1_ref[...], preferred_element_type=F32), 0.0)
        return jnp.dot((t * t).astype(BF16), w2_ref[...], preferred_element_type=F32)

    @pl.when(f == 0)
    def _():
        for rows in chunks:
            x = x_ref[rows, :]
            h = (x * _rms_scale(x) * g1_ref[...]).astype(h_ref.dtype)
            h_ref[rows, :] = h
            acc_ref[rows, :] = mlp(h)

    @pl.when(jnp.logical_and(f > 0, f < last))
    def _():
        acc_ref[...] += mlp(h_ref[...])

    @pl.when(f == last)
    def _():
        for rows in chunks:
            y = acc_ref[rows, :] + mlp(h_ref[rows, :])
            o_ref[rows, :] = x_ref[rows, :] + y * _rms_scale(y) * g2_ref[...]


def _ffn(x, g1, w1, w2, g2, *, cast=(), tm=512, rc=256, tf=1024):
    m, d = x.shape
    d_ff = w1.shape[1]
    nf = d_ff // tf
    assert nf >= 2, "the first and the last hidden slice are distinct grid steps"
    cast_in, cast_out, cast_shapes, cast_blocks = _side_cast_plan(cast, (m // tm) * nf, lambda i, f: i * nf + f)
    out = pl.pallas_call(
        functools.partial(_ffn_kernel, rc=rc, cast_blocks=tuple(cast_blocks)),
        out_shape=(jax.ShapeDtypeStruct((m, d), F32), *cast_shapes),
        grid=(m // tm, nf),
        in_specs=[
            pl.BlockSpec((tm, d), lambda i, f: (i, 0)),
            pl.BlockSpec((1, d), lambda i, f: (0, 0)),
            pl.BlockSpec((d, tf), lambda i, f: (0, f)),
            pl.BlockSpec((tf, d), lambda i, f: (f, 0)),
            pl.BlockSpec((1, d), lambda i, f: (0, 0)),
            *cast_in,
        ],
        out_specs=(pl.BlockSpec((tm, d), lambda i, f: (i, 0)), *cast_out),
        scratch_shapes=[pltpu.VMEM((tm, d), BF16), pltpu.VMEM((tm, d), F32)],
        compiler_params=_params("arbitrary", "arbitrary"),
        name="ffn",
    )(x, g1, w1, w2, g2, *[w_ for w_, _ in cast])
    return out if cast else out[0]


SB_ZERO_SUFFIX_BITS = 152.0


def _sb_kernel(*refs, blk, heads, cast_blocks):
    n_cast = len(cast_blocks)
    q_ref, k_ref, v_ref, tri_ref = refs[:4]
    o_ref = refs[4 + n_cast]
    step = (pl.program_id(0) * pl.num_programs(1) + pl.program_id(1)) * pl.num_programs(2) + pl.program_id(2)
    _side_cast(step, refs[4:4 + n_cast], refs[5 + n_cast:], cast_blocks)
    i = pl.program_id(2)
    tri = tri_ref[...]
    row = lax.broadcasted_iota(jnp.int32, (blk, blk), 0)
    col = lax.broadcasted_iota(jnp.int32, (blk, blk), 1)
    causal = col < row

    def blocks(jobs):
        cols = [slice(h * HEAD_DIM, (h + 1) * HEAD_DIM) for h, _, _, _ in jobs]
        starts = [pl.multiple_of(j * blk, blk) for _, j, _, _ in jobs]
        z2s = [lax.dot_general(q_ref[:, sl], k_ref[pl.ds(st, blk), sl], (((1,), (1,)), ((), ())),
                               preferred_element_type=F32) for sl, st in zip(cols, starts)]
        sps = [jnp.maximum(z2, 0.0) + jnp.log(1.0 + jnp.exp2(-jnp.abs(z2))) * LOG2_E for z2 in z2s]
        sps = [jnp.where(causal, sp, 0.0) if job[2] else sp for sp, job in zip(sps, jobs)]
        sums = [jnp.dot(sp.astype(BF16), tri, preferred_element_type=F32) for sp in sps]
        cums = []
        for part, (_, _, _, suffix) in zip(sums, jobs):
            if suffix is None:
                cums.append(part)
            else:
                cums.append(part + (cums[suffix][:, 0:1] if isinstance(suffix, int) else suffix))
        weights = [jnp.exp2(z2 - cum) for z2, cum in zip(z2s, cums)]
        weights = [jnp.where(causal, a, 0.0) if job[2] else a for a, job in zip(weights, jobs)]
        outs = [jnp.dot(a.astype(BF16), v_ref[pl.ds(st, blk), sl], preferred_element_type=F32)
                for a, sl, st in zip(weights, cols, starts)]
        return [(o, cum[:, 0:1]) for o, cum in zip(outs, cums)]

    def store(accs):
        for h in range(heads):
            o_ref[:, h * HEAD_DIM:(h + 1) * HEAD_DIM] = accs[h].astype(o_ref.dtype)

    @pl.when(i == 0)
    def _():
        store([o for o, _ in blocks([(h, 0, True, None) for h in range(heads)])])

    @pl.when(i > 0)
    def _():
        res = blocks([(h, i, True, None) for h in range(heads)]
                     + [(h, i - 1, False, h) for h in range(heads)])
        accs = tuple(res[h][0] + res[heads + h][0] for h in range(heads))
        sufs = tuple(res[heads + h][1] for h in range(heads))

        def cond(carry):
            j, _, sufs = carry
            smallest = functools.reduce(jnp.minimum, sufs)
            return jnp.logical_and(j >= 0, jnp.min(smallest) < SB_ZERO_SUFFIX_BITS)

        def body(carry):
            j, accs, sufs = carry
            new = blocks([(h, j, False, sufs[h]) for h in range(heads)])
            return j - 1, tuple(accs[h] + new[h][0] for h in range(heads)), tuple(n[1] for n in new)

        _, accs, _ = lax.while_loop(cond, body, (i - 2, accs, sufs))
        store(accs)


def _stick_breaking(qkv, bsz, t, *, n_heads, cast=(), blk=256, heads=4):
    nq = t // blk
    groups = n_heads // heads
    width = heads * HEAD_DIM
    ar = jnp.arange(blk)
    tri = (ar[:, None] >= ar[None, :]).astype(BF16)
    cast_in, cast_out, cast_shapes, cast_blocks = _side_cast_plan(
        cast, bsz * groups * nq, lambda b, g, i: (b * groups + g) * nq + i)
    return pl.pallas_call(
        functools.partial(_sb_kernel, blk=blk, heads=heads, cast_blocks=tuple(cast_blocks)),
        out_shape=(jax.ShapeDtypeStruct((bsz * t, n_heads * HEAD_DIM), BF16), *cast_shapes),
        grid=(bsz, groups, nq),
        in_specs=[
            pl.BlockSpec((blk, width), lambda b, g, i: (b * nq + i, g)),
            pl.BlockSpec((t, width), lambda b, g, i: (b, groups + g)),
            pl.BlockSpec((t, width), lambda b, g, i: (b, 2 * groups + g)),
            pl.BlockSpec((blk, blk), lambda b, g, i: (0, 0)),
            *cast_in,
        ],
        out_specs=(pl.BlockSpec((blk, width), lambda b, g, i: (b * nq + i, g)), *cast_out),
        compiler_params=_params("arbitrary", "arbitrary", "arbitrary"),
        name="stick_breaking",
    )(qkv, qkv, qkv, tri, *[w for w, _ in cast])


def kernel(x, norm_pre_mix, norm_post_mix, norm_pre_ffn, norm_post_ffn, ab_w_in, sgu_ln_g, sgu_ln_b, sgu_w, sgu_b,
           ab_w_out, sb_w_in, sb_w_out, ffn_w1, ffn_w2):
    bsz, t, d = x.shape
    a_width = sgu_ln_g.shape[-1]
    b_heads = (ab_w_in.shape[-1] - 2 * a_width) // (3 * HEAD_DIM)
    c_heads = sb_w_in.shape[-1] // (3 * HEAD_DIM)
    assert a_width == b_heads * HEAD_DIM, "dilated q/k/v column blocks are addressed in units of a_width"

    xs = x.reshape(bsz * t, d)
    row = lambda v: v.reshape(1, -1).astype(F32)

    assert tuple(dil for _, dil in DILATED_PAIRS) == (1, 4, 16), "proj_qkv0 regroups q/k/v for dilations 1, 4, 16"
    w_in = _to_bf16(ab_w_in, 0)
    uv, w1_0 = _proj_uv(xs, row(norm_pre_mix[0]), w_in, row(sgu_ln_g[0]), row(sgu_ln_b[0]), cast=[(ffn_w1, 0)])
    *grouped, w2_0, w_out0 = _proj_qkv0(xs, row(norm_pre_mix[0]), w_in, bsz, t, tn=a_width, first_block=2,
                                        q_scale=HEAD_DIM ** -0.5 * LOG2_E, cast=[(ffn_w2, 0), (ab_w_out, 0)])
    bias_full = jnp.repeat(sgu_b[0].T.astype(F32), CHUNK, axis=1)
    a_out = _sgu(uv, sgu_w[0], bias_full, a_width=a_width)
    outs, lses = zip(*[_dilated_branch(zd, t, window, dilation, n_heads=b_heads, qb=qb, rb=rb)
                       for zd, (window, dilation), (qb, rb) in zip(grouped, DILATED_PAIRS, DILATED_STEP)])
    lses = [lse.reshape(bsz * t, 128) for lse in lses]
    xs = _out_merge(a_out, outs, lses, w_out0, xs, row(norm_post_mix[0]))
    xs, w_in1 = _ffn(xs, row(norm_pre_ffn[0]), w1_0, w2_0, row(norm_post_ffn[0]), cast=[(sb_w_in, 0)])

    qkv = _norm_matmul(xs, row(norm_pre_mix[1]), w_in1, q_cols=c_heads * HEAD_DIM,
                       q_scale=HEAD_DIM ** -0.5 * LOG2_E)
    att, w_out1, w1_1, w2_1 = _stick_breaking(qkv, bsz, t, n_heads=c_heads,
                                              cast=[(sb_w_out, 0), (ffn_w1, 1), (ffn_w2, 1)])
    xs = _out_proj(att, w_out1, xs, row(norm_post_mix[1]))
    xs = _ffn(xs, row(norm_pre_ffn[1]), w1_1, w2_1, row(norm_post_ffn[1]))
    return xs.reshape(bsz, t, d)
```

```python
import functools
import math

import jax
import jax.numpy as jnp
from jax import lax
from jax.experimental import pallas as pl
from jax.experimental.pallas import tpu as pltpu

HEAD_DIM = 128
CHUNK = 128
ATT_BLOCK = 128
DILATED_PAIRS = ((128, 1), (512, 4), (2048, 16))
DILATED_STEP = ((8, 1), (2, 4), (1, 4))
RMS_EPS = 1e-6
LN_EPS = 1e-5
NEG_BIG = -1e30
LOG2_E = 1.4426950408889634

V7X_VMEM_LIMIT_BYTES = 56 * 1024 * 1024

BF16 = jnp.bfloat16
F32 = jnp.float32


def _params(*semantics):
    return pltpu.CompilerParams(dimension_semantics=semantics, vmem_limit_bytes=V7X_VMEM_LIMIT_BYTES)


def _const_spec(shape):
    return pl.BlockSpec(shape, lambda i: (0,) * len(shape), pipeline_mode=pl.Buffered(1))


def _cast_kernel(w_ref, o_ref):
    o_ref[...] = w_ref[...].astype(o_ref.dtype)


def _to_bf16(w, layer, *, block_bytes=8 * 1024 * 1024):
    _, r, c = w.shape
    br = r
    while 4 * br * c > block_bytes and br % 32 == 0:
        br //= 2
    return pl.pallas_call(
        _cast_kernel,
        out_shape=jax.ShapeDtypeStruct((r, c), BF16),
        grid=(r // br,),
        in_specs=[pl.BlockSpec((None, br, c), lambda i: (layer, i, 0))],
        out_specs=pl.BlockSpec((br, c), lambda i: (i, 0)),
        compiler_params=_params("parallel"),
        name="to_bf16",
    )(w)


def _side_cast_plan(weights, n_steps, step_of):
    in_specs, out_specs, out_shapes, n_blocks = [], [], [], []
    for w, layer in weights:
        _, r, c = w.shape
        nb = n_steps
        while nb > 1 and (r % nb or (r // nb) % 16):
            nb //= 2
        assert r % nb == 0 and (r // nb) % 16 == 0, (r, n_steps)
        rb = r // nb
        block = lambda *ids, nb=nb: jnp.minimum(step_of(*ids), nb - 1)
        in_specs.append(pl.BlockSpec((None, rb, c), lambda *ids, layer=layer, block=block: (layer, block(*ids), 0)))
        out_specs.append(pl.BlockSpec((rb, c), lambda *ids, block=block: (block(*ids), 0)))
        out_shapes.append(jax.ShapeDtypeStruct((r, c), BF16))
        n_blocks.append(nb)
    return in_specs, out_specs, out_shapes, n_blocks


def _side_cast(step, w_refs, o_refs, n_blocks):
    for w_ref, o_ref, nb in zip(w_refs, o_refs, n_blocks):
        @pl.when(step < nb)
        def _(w_ref=w_ref, o_ref=o_ref):
            o_ref[...] = w_ref[...].astype(o_ref.dtype)


def _rms_scale(x):
    return lax.rsqrt(jnp.mean(x * x, axis=-1, keepdims=True) + RMS_EPS)


def _gelu(x):
    return 0.5 * x * (1.0 + lax.erf(x * math.sqrt(0.5)))


def _normed_row_chunks(x_ref, g_ref, rc):
    for c in range(x_ref.shape[0] // rc):
        rows = slice(c * rc, (c + 1) * rc)
        x = x_ref[rows, :]
        yield c, rows, (x * _rms_scale(x) * g_ref[...]).astype(BF16)


def _norm_matmul_kernel(x_ref, g_ref, w_ref, o_ref, *, rc, tn, q_cols, q_scale):
    for _, rows, h in _normed_row_chunks(x_ref, g_ref, rc):
        for n in range(o_ref.shape[1] // tn):
            cols = slice(n * tn, (n + 1) * tn)
            acc = jnp.dot(h, w_ref[:, cols], preferred_element_type=F32)
            if (n + 1) * tn <= q_cols:
                acc = acc * q_scale
            o_ref[rows, cols] = acc.astype(o_ref.dtype)


def _norm_matmul(x, g, w, *, q_cols, q_scale, tm=512, rc=256, tn=1024):
    m, d = x.shape
    n = w.shape[1]
    assert q_cols % tn == 0
    return pl.pallas_call(
        functools.partial(_norm_matmul_kernel, rc=rc, tn=tn, q_cols=q_cols, q_scale=q_scale),
        out_shape=jax.ShapeDtypeStruct((m, n), BF16),
        grid=(m // tm,),
        in_specs=[pl.BlockSpec((tm, d), lambda i: (i, 0)), _const_spec((1, d)), _const_spec((d, n))],
        out_specs=pl.BlockSpec((tm, n), lambda i: (i, 0)),
        compiler_params=_params("parallel"),
        name="norm_matmul",
    )(x, g, w)


def _proj_uv_kernel(*refs, rc, cast_blocks):
    n_cast = len(cast_blocks)
    x_ref, g_ref, w_ref, lng_ref, lnb_ref = refs[:5]
    uv_ref = refs[5 + n_cast]
    _side_cast(pl.program_id(0), refs[5:5 + n_cast], refs[6 + n_cast:], cast_blocks)
    tn = lng_ref.shape[1]
    for _, rows, h in _normed_row_chunks(x_ref, g_ref, rc):
        v = _gelu(jnp.dot(h, w_ref[:, tn:], preferred_element_type=F32))
        mu = jnp.mean(v, axis=-1, keepdims=True)
        c = v - mu
        var = jnp.mean(c * c, axis=-1, keepdims=True)
        uv_ref[rows, tn:] = (c * lax.rsqrt(var + LN_EPS) * lng_ref[...] + lnb_ref[...]).astype(uv_ref.dtype)
        u = jnp.dot(h, w_ref[:, :tn], preferred_element_type=F32)
        uv_ref[rows, :tn] = _gelu(u).astype(uv_ref.dtype)


def _proj_uv(x, g, w, ln_g, ln_b, *, cast=(), tm=512, rc=256):
    m, d = x.shape
    n = 2 * ln_g.shape[1]
    cast_in, cast_out, cast_shapes, cast_blocks = _side_cast_plan(cast, m // tm, lambda i: i)
    return pl.pallas_call(
        functools.partial(_proj_uv_kernel, rc=rc, cast_blocks=tuple(cast_blocks)),
        out_shape=(jax.ShapeDtypeStruct((m, n), BF16), *cast_shapes),
        grid=(m // tm,),
        in_specs=[pl.BlockSpec((tm, d), lambda i: (i, 0)), _const_spec((1, d)), _const_spec((d, n)),
                  _const_spec(ln_g.shape), _const_spec(ln_b.shape), *cast_in],
        out_specs=(pl.BlockSpec((tm, n), lambda i: (i, 0)), *cast_out),
        compiler_params=_params("arbitrary"),
        name="proj_uv",
    )(x, g, w, ln_g, ln_b, *[w_ for w_, _ in cast])


def _proj_qkv0_kernel(*refs, rc, q_scale, cast_blocks):
    n_cast = len(cast_blocks)
    x_ref, g_ref, wq_ref, wk_ref, wv_ref = refs[:5]
    z1_ref, z4_ref, z16_ref = refs[5 + n_cast:8 + n_cast]
    s_ref, s4_ref = refs[8 + 2 * n_cast:]
    _side_cast(pl.program_id(0), refs[5:5 + n_cast], refs[8 + n_cast:8 + 2 * n_cast], cast_blocks)
    tn = wq_ref.shape[1]
    slabs = tn // 128
    q4, q16 = rc // 4, rc // 16
    for c, rows, h in _normed_row_chunks(x_ref, g_ref, rc):
        rows4 = slice(c * q4, (c + 1) * q4)
        rows16 = slice(c * q16, (c + 1) * q16)
        for part, w_ref in enumerate((wq_ref, wk_ref, wv_ref)):
            acc = jnp.dot(h, w_ref[...], preferred_element_type=F32)
            if part == 0:
                acc = acc * q_scale
            z1_ref[rows, part * tn:(part + 1) * tn] = acc.astype(z1_ref.dtype)
            for s in range(slabs):
                s_ref[s] = acc[:, s * 128:(s + 1) * 128]
            for r in range(4):
                for s in range(slabs):
                    col = (part * 4 + r) * tn + s * 128
                    piece = s_ref[s, pl.ds(r, q4, stride=4), :]
                    z4_ref[rows4, col:col + 128] = piece.astype(z4_ref.dtype)
                    s4_ref[s, r * q4:(r + 1) * q4, :] = piece
            for r in range(16):
                for s in range(slabs):
                    col = (part * 16 + r) * tn + s * 128
                    piece = s4_ref[s, pl.ds((r % 4) * q4 + r // 4, q16, stride=4), :]
                    z16_ref[rows16, col:col + 128] = piece.astype(z16_ref.dtype)


def _proj_qkv0(x, g, w, bsz, t, *, tn, first_block, q_scale, cast=(), tm=512, rc=256):
    m, d = x.shape
    n = 3 * tn
    tiles = t // tm
    grouped = lambda dil: pl.BlockSpec((None, tm // dil, dil * n), lambda i: (i // tiles, i % tiles, 0))
    w_block = lambda part: pl.BlockSpec((d, tn), lambda i: (0, first_block + part), pipeline_mode=pl.Buffered(1))
    cast_in, cast_out, cast_shapes, cast_blocks = _side_cast_plan(cast, m // tm, lambda i: i)
    return pl.pallas_call(
        functools.partial(_proj_qkv0_kernel, rc=rc, q_scale=q_scale, cast_blocks=tuple(cast_blocks)),
        out_shape=(*[jax.ShapeDtypeStruct((bsz, t // dil, dil * n), BF16) for dil in (1, 4, 16)], *cast_shapes),
        grid=(m // tm,),
        in_specs=[pl.BlockSpec((tm, d), lambda i: (i, 0)), _const_spec((1, d)), w_block(0), w_block(1), w_block(2),
                  *cast_in],
        out_specs=(grouped(1), grouped(4), grouped(16), *cast_out),
        scratch_shapes=[pltpu.VMEM((tn // 128, rc, 128), F32), pltpu.VMEM((tn // 128, rc, 128), F32)],
        compiler_params=_params("arbitrary"),
        name="proj_qkv0",
    )(x, g, w, w, w, *[w_ for w_, _ in cast])


def _sgu_kernel(u_ref, v_ref, w_ref, b_ref, o_ref, *, chunks, groups):
    row = lax.broadcasted_iota(jnp.int32, (CHUNK, CHUNK), 0)
    col = lax.broadcasted_iota(jnp.int32, (CHUNK, CHUNK), 1)
    for g in range(groups):
        cs = slice(g * CHUNK, (g + 1) * CHUNK)
        wg = jnp.where(row >= col, w_ref[g], 0.0).astype(BF16)
        bias = b_ref[:, cs]
        for c in range(chunks):
            rs = slice(c * CHUNK, (c + 1) * CHUNK)
            mixed = jnp.dot(wg, v_ref[rs, cs], preferred_element_type=F32) + bias
            o_ref[rs, cs] = (u_ref[rs, cs].astype(F32) * mixed).astype(o_ref.dtype)


def _sgu(z, w_s, bias_full, *, a_width, chunks=16):
    m = z.shape[0]
    groups = w_s.shape[0]
    tm = chunks * CHUNK
    return pl.pallas_call(
        functools.partial(_sgu_kernel, chunks=chunks, groups=groups),
        out_shape=jax.ShapeDtypeStruct((m, a_width), BF16),
        grid=(m // tm,),
        in_specs=[
            pl.BlockSpec((tm, a_width), lambda i: (i, 0)),
            pl.BlockSpec((tm, a_width), lambda i: (i, 1)),
            pl.BlockSpec((groups, CHUNK, CHUNK), lambda i: (0, 0, 0)),
            pl.BlockSpec((CHUNK, a_width), lambda i: (0, 0)),
        ],
        out_specs=pl.BlockSpec((tm, a_width), lambda i: (i, 0)),
        compiler_params=_params("parallel"),
        name="sgu",
    )(z, z, w_s, bias_full)


def _dilated_kernel(q_ref, kp_ref, kc_ref, vp_ref, vc_ref, bias_ref, o_ref, lse_ref, *, n_heads, dilation, qb, rb):
    blk = ATT_BLOCK
    width = n_heads * HEAD_DIM
    at_start = pl.program_id(1) == 0
    lane = lax.broadcasted_iota(jnp.int32, (blk, 128), 1)
    ones = jnp.ones((2 * blk, HEAD_DIM), BF16)
    for u, rr in [(u, rr) for u in range(qb) for rr in range(rb)]:
        first = jnp.where(at_start, 0, 1) if u == 0 else 1
        start = u * blk * dilation + pl.program_id(2) * rb + rr
        rows = pl.ds(start, blk, stride=dilation) if dilation > 1 else slice(u * blk, (u + 1) * blk)
        cur = slice(u * blk, (u + 1) * blk)
        before = slice((u - 1) * blk, u * blk)
        lse = jnp.zeros((blk, 128), F32)
        for h in range(n_heads):
            sl = slice(rr * width + h * HEAD_DIM, rr * width + (h + 1) * HEAD_DIM)
            k_prev = kp_ref[:, sl] if u == 0 else kc_ref[before, sl]
            v_prev = vp_ref[:, sl] if u == 0 else vc_ref[before, sl]
            k = jnp.concatenate([k_prev, kc_ref[cur, sl]], axis=0)
            v = jnp.concatenate([v_prev, vc_ref[cur, sl]], axis=0)
            s = lax.dot_general(q_ref[cur, sl], k, (((1,), (1,)), ((), ())), preferred_element_type=F32)
            s = s + bias_ref[first, h]
            m = jnp.max(s, axis=-1, keepdims=True)
            p = jnp.exp2(s - m).astype(BF16)
            pv = jnp.dot(p, jnp.concatenate([v, ones], axis=1), preferred_element_type=F32)
            den = pv[:, HEAD_DIM:]
            o_ref[h, rows, :] = (pv[:, :HEAD_DIM] * (1.0 / den)).astype(o_ref.dtype)
            lse = jnp.where(lane == h, m + jnp.log(den) * LOG2_E, lse)
        lse_ref[rows, :] = lse


def _dilated_bias(window, dilation, n_heads):
    blk = ATT_BLOCK
    win_sub = window // dilation
    qi = jnp.arange(blk)[:, None]
    kj = jnp.arange(2 * blk)[None, :]
    dist = qi + blk - kj
    band = (dist >= 0) & (dist <= win_sub)
    slopes = jnp.exp2(-8.0 * (jnp.arange(n_heads, dtype=F32) + 1.0) / n_heads)
    bias = -slopes[:, None, None] * (dist * dilation).astype(F32)[None] * LOG2_E
    later = jnp.where(band[None], bias, NEG_BIG)
    first = jnp.where((band & (kj >= blk))[None], bias, NEG_BIG)
    return jnp.stack([first, later], axis=0)


def _dilated_branch(zd, t, window, dilation, *, n_heads, qb, rb):
    bsz = zd.shape[0]
    width = n_heads * HEAD_DIM
    d = dilation
    steps = t // (d * ATT_BLOCK * qb)
    span = qb * d * ATT_BLOCK
    groups = d // rb
    bias = _dilated_bias(window, d, n_heads)

    def cur(part):
        return pl.BlockSpec((None, qb * ATT_BLOCK, rb * width), lambda b, n, r: (b, n, part * groups + r))

    def prev(part):
        return pl.BlockSpec((None, ATT_BLOCK, rb * width),
                            lambda b, n, r: (b, jnp.maximum(qb * n - 1, 0), part * groups + r))

    return pl.pallas_call(
        functools.partial(_dilated_kernel, n_heads=n_heads, dilation=d, qb=qb, rb=rb),
        out_shape=(jax.ShapeDtypeStruct((bsz, n_heads, t, HEAD_DIM), BF16 if d == 1 else F32),
                   jax.ShapeDtypeStruct((bsz, t, 128), F32)),
        grid=(bsz, steps, groups),
        in_specs=[cur(0), prev(1), cur(1), prev(2), cur(2),
                  pl.BlockSpec((2, n_heads, ATT_BLOCK, 2 * ATT_BLOCK), lambda b, n, r: (0, 0, 0, 0))],
        out_specs=(pl.BlockSpec((None, n_heads, span, HEAD_DIM), lambda b, n, r: (b, 0, n, 0)),
                   pl.BlockSpec((None, span, 128), lambda b, n, r: (b, n, 0))),
        compiler_params=_params("parallel", "arbitrary", "arbitrary"),
        name=f"dilated_d{d}",
    )(zd, zd, zd, zd, zd, bias)


def _out_merge_kernel(a_ref, o1_ref, o2_ref, o3_ref, l1_ref, l2_ref, l3_ref, w_ref, x_ref, g_ref, out_ref,
                      *, n_heads, a_width, rc):
    for c in range(x_ref.shape[0] // rc):
        rows = slice(c * rc, (c + 1) * rc)
        l1, l2, l3 = l1_ref[rows, :], l2_ref[rows, :], l3_ref[rows, :]
        m = jnp.maximum(jnp.maximum(l1, l2), l3)
        e1, e2, e3 = jnp.exp2(l1 - m), jnp.exp2(l2 - m), jnp.exp2(l3 - m)
        inv = 1.0 / (e1 + e2 + e3)
        w1, w2, w3 = e1 * inv, e2 * inv, e3 * inv
        parts = []
        for h in range(n_heads):
            bh = (w1[:, h:h + 1] * o1_ref[h, rows, :] + w2[:, h:h + 1] * o2_ref[h, rows, :]
                  + w3[:, h:h + 1] * o3_ref[h, rows, :])
            parts.append(bh.astype(BF16))
        b_out = jnp.concatenate(parts, axis=1)
        y = (jnp.dot(a_ref[rows, :], w_ref[:a_width, :], preferred_element_type=F32)
             + jnp.dot(b_out, w_ref[a_width:, :], preferred_element_type=F32))
        out_ref[rows, :] = x_ref[rows, :] + y * _rms_scale(y) * g_ref[...]


def _out_merge(a_out, outs, lses, w_out, x, g, *, tm=512, rc=256):
    m, d = x.shape
    a_width = a_out.shape[1]
    bsz, n_heads, t, _ = outs[0].shape
    tiles = t // tm
    row = lambda width: pl.BlockSpec((tm, width), lambda i: (i, 0))
    heads = pl.BlockSpec((None, n_heads, tm, HEAD_DIM), lambda i: (i // tiles, 0, i % tiles, 0))
    return pl.pallas_call(
        functools.partial(_out_merge_kernel, n_heads=n_heads, a_width=a_width, rc=rc),
        out_shape=jax.ShapeDtypeStruct((m, d), F32),
        grid=(m // tm,),
        in_specs=[row(a_width), heads, heads, heads, row(128), row(128), row(128),
                  _const_spec(w_out.shape), row(d), _const_spec((1, d))],
        out_specs=row(d),
        compiler_params=_params("parallel"),
        name="out_merge",
    )(a_out, *outs, *lses, w_out, x, g)


def _out_proj_kernel(a_ref, w_ref, x_ref, g_ref, out_ref, *, rc):
    for c in range(x_ref.shape[0] // rc):
        rows = slice(c * rc, (c + 1) * rc)
        y = jnp.dot(a_ref[rows, :], w_ref[...], preferred_element_type=F32)
        out_ref[rows, :] = x_ref[rows, :] + y * _rms_scale(y) * g_ref[...]


def _out_proj(a, w_out, x, g, *, tm=512, rc=256):
    m, d = x.shape
    row = lambda width: pl.BlockSpec((tm, width), lambda i: (i, 0))
    return pl.pallas_call(
        functools.partial(_out_proj_kernel, rc=rc),
        out_shape=jax.ShapeDtypeStruct((m, d), F32),
        grid=(m // tm,),
        in_specs=[row(a.shape[1]), _const_spec(w_out.shape), row(d), _const_spec((1, d))],
        out_specs=row(d),
        compiler_params=_params("parallel"),
        name="out_proj",
    )(a, w_out, x, g)


def _ffn_kernel(x_ref, g1_ref, w1_ref, w2_ref, g2_ref, o_ref, h_ref, acc_ref, *, rc):
    f = pl.program_id(1)
    last = pl.num_programs(1) - 1
    chunks = [slice(c * rc, (c + 1) * rc) for c in range(x_ref.shape[0] // rc)]

    def mlp(h):
        t = jnp.maximum(jnp.dot(h, w1_ref[...], preferred_element_type=F32), 0.0)
        return jnp.dot((t * t).astype(BF16), w2_ref[...], preferred_element_type=F32)

    @pl.when(f == 0)
    def _():
        for rows in chunks:
            x = x_ref[rows, :]
            h = (x * _rms_scale(x) * g1_ref[...]).astype(h_ref.dtype)
            h_ref[rows, :] = h
            acc_ref[rows, :] = mlp(h)

    @pl.when(jnp.logical_and(f > 0, f < last))
    def _():
        acc_ref[...] += mlp(h_ref[...])

    @pl.when(f == last)
    def _():
        for rows in chunks:
            y = acc_ref[rows, :] + mlp(h_ref[rows, :])
            o_ref[rows, :] = x_ref[rows, :] + y * _rms_scale(y) * g2_ref[...]


def _ffn(x, g1, w1, w2, g2, *, tm=512, rc=256, tf=1024):
    m, d = x.shape
    d_ff = w1.shape[1]
    assert d_ff // tf >= 2, "the first and the last hidden slice are distinct grid steps"
    return pl.pallas_call(
        functools.partial(_ffn_kernel, rc=rc),
        out_shape=jax.ShapeDtypeStruct((m, d), F32),
        grid=(m // tm, d_ff // tf),
        in_specs=[
            pl.BlockSpec((tm, d), lambda i, f: (i, 0)),
            pl.BlockSpec((1, d), lambda i, f: (0, 0)),
            pl.BlockSpec((d, tf), lambda i, f: (0, f)),
            pl.BlockSpec((tf, d), lambda i, f: (f, 0)),
            pl.BlockSpec((1, d), lambda i, f: (0, 0)),
        ],
        out_specs=pl.BlockSpec((tm, d), lambda i, f: (i, 0)),
        scratch_shapes=[pltpu.VMEM((tm, d), BF16), pltpu.VMEM((tm, d), F32)],
        compiler_params=_params("parallel", "arbitrary"),
        name="ffn",
    )(x, g1, w1, w2, g2)


SB_ZERO_SUFFIX_BITS = 152.0


def _sb_kernel(*refs, blk, heads, cast_blocks):
    n_cast = len(cast_blocks)
    q_ref, k_ref, v_ref, tri_ref = refs[:4]
    o_ref = refs[4 + n_cast]
    step = (pl.program_id(0) * pl.num_programs(1) + pl.program_id(1)) * pl.num_programs(2) + pl.program_id(2)
    _side_cast(step, refs[4:4 + n_cast], refs[5 + n_cast:], cast_blocks)
    i = pl.program_id(2)
    tri = tri_ref[...]
    row = lax.broadcasted_iota(jnp.int32, (blk, blk), 0)
    col = lax.broadcasted_iota(jnp.int32, (blk, blk), 1)
    causal = col < row

    def blocks(jobs):
        cols = [slice(h * HEAD_DIM, (h + 1) * HEAD_DIM) for h, _, _, _ in jobs]
        starts = [pl.multiple_of(j * blk, blk) for _, j, _, _ in jobs]
        z2s = [lax.dot_general(q_ref[:, sl], k_ref[pl.ds(st, blk), sl], (((1,), (1,)), ((), ())),
                               preferred_element_type=F32) for sl, st in zip(cols, starts)]
        sps = [jnp.maximum(z2, 0.0) + jnp.log(1.0 + jnp.exp2(-jnp.abs(z2))) * LOG2_E for z2 in z2s]
        sps = [jnp.where(causal, sp, 0.0) if job[2] else sp for sp, job in zip(sps, jobs)]
        sums = [jnp.dot(sp.astype(BF16), tri, preferred_element_type=F32) for sp in sps]
        cums = []
        for part, (_, _, _, suffix) in zip(sums, jobs):
            if suffix is None:
                cums.append(part)
            else:
                cums.append(part + (cums[suffix][:, 0:1] if isinstance(suffix, int) else suffix))
        weights = [jnp.exp2(z2 - cum) for z2, cum in zip(z2s, cums)]
        weights = [jnp.where(causal, a, 0.0) if job[2] else a for a, job in zip(weights, jobs)]
        outs = [jnp.dot(a.astype(BF16), v_ref[pl.ds(st, blk), sl], preferred_element_type=F32)
                for a, sl, st in zip(weights, cols, starts)]
        return [(o, cum[:, 0:1]) for o, cum in zip(outs, cums)]

    def store(accs):
        for h in range(heads):
            o_ref[:, h * HEAD_DIM:(h + 1) * HEAD_DIM] = accs[h].astype(o_ref.dtype)

    @pl.when(i == 0)
    def _():
        store([o for o, _ in blocks([(h, 0, True, None) for h in range(heads)])])

    @pl.when(i > 0)
    def _():
        res = blocks([(h, i, True, None) for h in range(heads)]
                     + [(h, i - 1, False, h) for h in range(heads)])
        accs = tuple(res[h][0] + res[heads + h][0] for h in range(heads))
        sufs = tuple(res[heads + h][1] for h in range(heads))

        def cond(carry):
            j, _, sufs = carry
            smallest = functools.reduce(jnp.minimum, sufs)
            return jnp.logical_and(j >= 0, jnp.min(smallest) < SB_ZERO_SUFFIX_BITS)

        def body(carry):
            j, accs, sufs = carry
            new = blocks([(h, j, False, sufs[h]) for h in range(heads)])
            return j - 1, tuple(accs[h] + new[h][0] for h in range(heads)), tuple(n[1] for n in new)

        _, accs, _ = lax.while_loop(cond, body, (i - 2, accs, sufs))
        store(accs)


def _stick_breaking(qkv, bsz, t, *, n_heads, cast=(), blk=256, heads=4):
    nq = t // blk
    groups = n_heads // heads
    width = heads * HEAD_DIM
    ar = jnp.arange(blk)
    tri = (ar[:, None] >= ar[None, :]).astype(BF16)
    cast_in, cast_out, cast_shapes, cast_blocks = _side_cast_plan(
        cast, bsz * groups * nq, lambda b, g, i: (b * groups + g) * nq + i)
    return pl.pallas_call(
        functools.partial(_sb_kernel, blk=blk, heads=heads, cast_blocks=tuple(cast_blocks)),
        out_shape=(jax.ShapeDtypeStruct((bsz * t, n_heads * HEAD_DIM), BF16), *cast_shapes),
        grid=(bsz, groups, nq),
        in_specs=[
            pl.BlockSpec((blk, width), lambda b, g, i: (b * nq + i, g)),
            pl.BlockSpec((t, width), lambda b, g, i: (b, groups + g)),
            pl.BlockSpec((t, width), lambda b, g, i: (b, 2 * groups + g)),
            pl.BlockSpec((blk, blk), lambda b, g, i: (0, 0)),
            *cast_in,
        ],
        out_specs=(pl.BlockSpec((blk, width), lambda b, g, i: (b * nq + i, g)), *cast_out),
        compiler_params=_params("arbitrary", "arbitrary", "arbitrary"),
        name="stick_breaking",
    )(qkv, qkv, qkv, tri, *[w for w, _ in cast])


def kernel(x, norm_pre_mix, norm_post_mix, norm_pre_ffn, norm_post_ffn, ab_w_in, sgu_ln_g, sgu_ln_b, sgu_w, sgu_b,
           ab_w_out, sb_w_in, sb_w_out, ffn_w1, ffn_w2):
    bsz, t, d = x.shape
    a_width = sgu_ln_g.shape[-1]
    b_heads = (ab_w_in.shape[-1] - 2 * a_width) // (3 * HEAD_DIM)
    c_heads = sb_w_in.shape[-1] // (3 * HEAD_DIM)
    assert a_width == b_heads * HEAD_DIM, "dilated q/k/v column blocks are addressed in units of a_width"

    xs = x.reshape(bsz * t, d)
    row = lambda v: v.reshape(1, -1).astype(F32)

    assert tuple(dil for _, dil in DILATED_PAIRS) == (1, 4, 16), "proj_qkv0 regroups q/k/v for dilations 1, 4, 16"
    w_in = _to_bf16(ab_w_in, 0)
    uv, w1_0, w_in1 = _proj_uv(xs, row(norm_pre_mix[0]), w_in, row(sgu_ln_g[0]), row(sgu_ln_b[0]),
                               cast=[(ffn_w1, 0), (sb_w_in, 0)])
    *grouped, w2_0, w_out0 = _proj_qkv0(xs, row(norm_pre_mix[0]), w_in, bsz, t, tn=a_width, first_block=2,
                                        q_scale=HEAD_DIM ** -0.5 * LOG2_E, cast=[(ffn_w2, 0), (ab_w_out, 0)])
    bias_full = jnp.repeat(sgu_b[0].T.astype(F32), CHUNK, axis=1)
    a_out = _sgu(uv, sgu_w[0], bias_full, a_width=a_width)
    outs, lses = zip(*[_dilated_branch(zd, t, window, dilation, n_heads=b_heads, qb=qb, rb=rb)
                       for zd, (window, dilation), (qb, rb) in zip(grouped, DILATED_PAIRS, DILATED_STEP)])
    lses = [lse.reshape(bsz * t, 128) for lse in lses]
    xs = _out_merge(a_out, outs, lses, w_out0, xs, row(norm_post_mix[0]))
    xs = _ffn(xs, row(norm_pre_ffn[0]), w1_0, w2_0, row(norm_post_ffn[0]))

    qkv = _norm_matmul(xs, row(norm_pre_mix[1]), w_in1, q_cols=c_heads * HEAD_DIM,
                       q_scale=HEAD_DIM ** -0.5 * LOG2_E)
    att, w_out1, w1_1, w2_1 = _stick_breaking(qkv, bsz, t, n_heads=c_heads,
                                              cast=[(sb_w_out, 0), (ffn_w1, 1), (ffn_w2, 1)])
    xs = _out_proj(att, w_out1, xs, row(norm_post_mix[1]))
    xs = _ffn(xs, row(norm_pre_ffn[1]), w1_1, w2_1, row(norm_post_ffn[1]))
    return xs.reshape(bsz, t, d)
```

```python
import functools
import math

import jax
import jax.numpy as jnp
from jax import lax
from jax.experimental import pallas as pl
from jax.experimental.pallas import tpu as pltpu

HEAD_DIM = 128
CHUNK = 128
ATT_BLOCK = 128
DILATED_PAIRS = ((128, 1), (512, 4), (2048, 16))
DILATED_STEP = ((16, 1), (2, 4), (1, 8))
RMS_EPS = 1e-6
LN_EPS = 1e-5
NEG_BIG = -1e30
LOG2_E = 1.4426950408889634

V7X_VMEM_LIMIT_BYTES = 56 * 1024 * 1024

BF16 = jnp.bfloat16
F32 = jnp.float32


def _params(*semantics):
    return pltpu.CompilerParams(dimension_semantics=semantics, vmem_limit_bytes=V7X_VMEM_LIMIT_BYTES)


def _const_spec(shape):
    return pl.BlockSpec(shape, lambda i: (0,) * len(shape), pipeline_mode=pl.Buffered(1))


def _cast_kernel(w_ref, o_ref):
    o_ref[...] = w_ref[...].astype(o_ref.dtype)


def _to_bf16(w, layer, *, block_bytes=8 * 1024 * 1024):
    _, r, c = w.shape
    br = r
    while 4 * br * c > block_bytes and br % 32 == 0:
        br //= 2
    return pl.pallas_call(
        _cast_kernel,
        out_shape=jax.ShapeDtypeStruct((r, c), BF16),
        grid=(r // br,),
        in_specs=[pl.BlockSpec((None, br, c), lambda i: (layer, i, 0))],
        out_specs=pl.BlockSpec((br, c), lambda i: (i, 0)),
        compiler_params=_params("parallel"),
        name="to_bf16",
    )(w)


def _side_cast_plan(weights, n_steps, step_of):
    in_specs, out_specs, out_shapes, n_blocks = [], [], [], []
    for w, layer in weights:
        _, r, c = w.shape
        nb = n_steps
        while nb > 1 and (r % nb or (r // nb) % 16):
            nb //= 2
        assert r % nb == 0 and (r // nb) % 16 == 0, (r, n_steps)
        rb = r // nb
        block = lambda *ids, nb=nb: jnp.minimum(step_of(*ids), nb - 1)
        in_specs.append(pl.BlockSpec((None, rb, c), lambda *ids, layer=layer, block=block: (layer, block(*ids), 0)))
        out_specs.append(pl.BlockSpec((rb, c), lambda *ids, block=block: (block(*ids), 0)))
        out_shapes.append(jax.ShapeDtypeStruct((r, c), BF16))
        n_blocks.append(nb)
    return in_specs, out_specs, out_shapes, n_blocks


def _side_cast(step, w_refs, o_refs, n_blocks):
    for w_ref, o_ref, nb in zip(w_refs, o_refs, n_blocks):
        @pl.when(step < nb)
        def _(w_ref=w_ref, o_ref=o_ref):
            o_ref[...] = w_ref[...].astype(o_ref.dtype)


def _rms_scale(x):
    return lax.rsqrt(jnp.mean(x * x, axis=-1, keepdims=True) + RMS_EPS)


def _gelu(x):
    return 0.5 * x * (1.0 + lax.erf(x * math.sqrt(0.5)))


def _normed_row_chunks(x_ref, g_ref, rc):
    for c in range(x_ref.shape[0] // rc):
        rows = slice(c * rc, (c + 1) * rc)
        x = x_ref[rows, :]
        yield c, rows, (x * _rms_scale(x) * g_ref[...]).astype(BF16)


def _norm_matmul_kernel(x_ref, g_ref, w_ref, o_ref, *, rc, tn, q_cols, q_scale):
    for _, rows, h in _normed_row_chunks(x_ref, g_ref, rc):
        for n in range(o_ref.shape[1] // tn):
            cols = slice(n * tn, (n + 1) * tn)
            acc = jnp.dot(h, w_ref[:, cols], preferred_element_type=F32)
            if (n + 1) * tn <= q_cols:
                acc = acc * q_scale
            o_ref[rows, cols] = acc.astype(o_ref.dtype)


def _norm_matmul(x, g, w, *, q_cols, q_scale, tm=512, rc=256, tn=1024):
    m, d = x.shape
    n = w.shape[1]
    assert q_cols % tn == 0
    return pl.pallas_call(
        functools.partial(_norm_matmul_kernel, rc=rc, tn=tn, q_cols=q_cols, q_scale=q_scale),
        out_shape=jax.ShapeDtypeStruct((m, n), BF16),
        grid=(m // tm,),
        in_specs=[pl.BlockSpec((tm, d), lambda i: (i, 0)), _const_spec((1, d)), _const_spec((d, n))],
        out_specs=pl.BlockSpec((tm, n), lambda i: (i, 0)),
        compiler_params=_params("parallel"),
        name="norm_matmul",
    )(x, g, w)


def _proj_uv_kernel(*refs, rc, cast_blocks):
    n_cast = len(cast_blocks)
    x_ref, g_ref, w_ref, lng_ref, lnb_ref = refs[:5]
    uv_ref = refs[5 + n_cast]
    _side_cast(pl.program_id(0), refs[5:5 + n_cast], refs[6 + n_cast:], cast_blocks)
    tn = lng_ref.shape[1]
    for _, rows, h in _normed_row_chunks(x_ref, g_ref, rc):
        v = _gelu(jnp.dot(h, w_ref[:, tn:], preferred_element_type=F32))
        mu = jnp.mean(v, axis=-1, keepdims=True)
        c = v - mu
        var = jnp.mean(c * c, axis=-1, keepdims=True)
        uv_ref[rows, tn:] = (c * lax.rsqrt(var + LN_EPS) * lng_ref[...] + lnb_ref[...]).astype(uv_ref.dtype)
        u = jnp.dot(h, w_ref[:, :tn], preferred_element_type=F32)
        uv_ref[rows, :tn] = _gelu(u).astype(uv_ref.dtype)


def _proj_uv(x, g, w, ln_g, ln_b, *, cast=(), tm=512, rc=256):
    m, d = x.shape
    n = 2 * ln_g.shape[1]
    cast_in, cast_out, cast_shapes, cast_blocks = _side_cast_plan(cast, m // tm, lambda i: i)
    return pl.pallas_call(
        functools.partial(_proj_uv_kernel, rc=rc, cast_blocks=tuple(cast_blocks)),
        out_shape=(jax.ShapeDtypeStruct((m, n), BF16), *cast_shapes),
        grid=(m // tm,),
        in_specs=[pl.BlockSpec((tm, d), lambda i: (i, 0)), _const_spec((1, d)), _const_spec((d, n)),
                  _const_spec(ln_g.shape), _const_spec(ln_b.shape), *cast_in],
        out_specs=(pl.BlockSpec((tm, n), lambda i: (i, 0)), *cast_out),
        compiler_params=_params("arbitrary"),
        name="proj_uv",
    )(x, g, w, ln_g, ln_b, *[w_ for w_, _ in cast])


def _proj_qkv0_kernel(*refs, rc, q_scale, cast_blocks):
    n_cast = len(cast_blocks)
    x_ref, g_ref, wq_ref, wk_ref, wv_ref = refs[:5]
    z1_ref, z4_ref, z16_ref = refs[5 + n_cast:8 + n_cast]
    s_ref, s4_ref = refs[8 + 2 * n_cast:]
    _side_cast(pl.program_id(0), refs[5:5 + n_cast], refs[8 + n_cast:8 + 2 * n_cast], cast_blocks)
    tn = wq_ref.shape[1]
    slabs = tn // 128
    q4, q16 = rc // 4, rc // 16
    for c, rows, h in _normed_row_chunks(x_ref, g_ref, rc):
        rows4 = slice(c * q4, (c + 1) * q4)
        rows16 = slice(c * q16, (c + 1) * q16)
        for part, w_ref in enumerate((wq_ref, wk_ref, wv_ref)):
            acc = jnp.dot(h, w_ref[...], preferred_element_type=F32)
            if part == 0:
                acc = acc * q_scale
            z1_ref[rows, part * tn:(part + 1) * tn] = acc.astype(z1_ref.dtype)
            for s in range(slabs):
                s_ref[s] = acc[:, s * 128:(s + 1) * 128]
            for r in range(4):
                for s in range(slabs):
                    col = (part * 4 + r) * tn + s * 128
                    piece = s_ref[s, pl.ds(r, q4, stride=4), :]
                    z4_ref[rows4, col:col + 128] = piece.astype(z4_ref.dtype)
                    s4_ref[s, r * q4:(r + 1) * q4, :] = piece
            for r in range(16):
                for s in range(slabs):
                    col = (part * 16 + r) * tn + s * 128
                    piece = s4_ref[s, pl.ds((r % 4) * q4 + r // 4, q16, stride=4), :]
                    z16_ref[rows16, col:col + 128] = piece.astype(z16_ref.dtype)


def _proj_qkv0(x, g, w, bsz, t, *, tn, first_block, q_scale, cast=(), tm=512, rc=256):
    m, d = x.shape
    n = 3 * tn
    tiles = t // tm
    grouped = lambda dil: pl.BlockSpec((None, tm // dil, dil * n), lambda i: (i // tiles, i % tiles, 0))
    w_block = lambda part: pl.BlockSpec((d, tn), lambda i: (0, first_block + part), pipeline_mode=pl.Buffered(1))
    cast_in, cast_out, cast_shapes, cast_blocks = _side_cast_plan(cast, m // tm, lambda i: i)
    return pl.pallas_call(
        functools.partial(_proj_qkv0_kernel, rc=rc, q_scale=q_scale, cast_blocks=tuple(cast_blocks)),
        out_shape=(*[jax.ShapeDtypeStruct((bsz, t // dil, dil * n), BF16) for dil in (1, 4, 16)], *cast_shapes),
        grid=(m // tm,),
        in_specs=[pl.BlockSpec((tm, d), lambda i: (i, 0)), _const_spec((1, d)), w_block(0), w_block(1), w_block(2),
                  *cast_in],
        out_specs=(grouped(1), grouped(4), grouped(16), *cast_out),
        scratch_shapes=[pltpu.VMEM((tn // 128, rc, 128), F32), pltpu.VMEM((tn // 128, rc, 128), F32)],
        compiler_params=_params("arbitrary"),
        name="proj_qkv0",
    )(x, g, w, w, w, *[w_ for w_, _ in cast])


def _sgu_kernel(u_ref, v_ref, w_ref, b_ref, o_ref, *, chunks, groups):
    row = lax.broadcasted_iota(jnp.int32, (CHUNK, CHUNK), 0)
    col = lax.broadcasted_iota(jnp.int32, (CHUNK, CHUNK), 1)
    for g in range(groups):
        cs = slice(g * CHUNK, (g + 1) * CHUNK)
        wg = jnp.where(row >= col, w_ref[g], 0.0).astype(BF16)
        bias = b_ref[:, cs]
        for c in range(chunks):
            rs = slice(c * CHUNK, (c + 1) * CHUNK)
            mixed = jnp.dot(wg, v_ref[rs, cs], preferred_element_type=F32) + bias
            o_ref[rs, cs] = (u_ref[rs, cs].astype(F32) * mixed).astype(o_ref.dtype)


def _sgu(z, w_s, bias_full, *, a_width, chunks=16):
    m = z.shape[0]
    groups = w_s.shape[0]
    tm = chunks * CHUNK
    return pl.pallas_call(
        functools.partial(_sgu_kernel, chunks=chunks, groups=groups),
        out_shape=jax.ShapeDtypeStruct((m, a_width), BF16),
        grid=(m // tm,),
        in_specs=[
            pl.BlockSpec((tm, a_width), lambda i: (i, 0)),
            pl.BlockSpec((tm, a_width), lambda i: (i, 1)),
            pl.BlockSpec((groups, CHUNK, CHUNK), lambda i: (0, 0, 0)),
            pl.BlockSpec((CHUNK, a_width), lambda i: (0, 0)),
        ],
        out_specs=pl.BlockSpec((tm, a_width), lambda i: (i, 0)),
        compiler_params=_params("parallel"),
        name="sgu",
    )(z, z, w_s, bias_full)


def _dilated_kernel(q_ref, kp_ref, kc_ref, vp_ref, vc_ref, bias_ref, o_ref, lse_ref, *, n_heads, dilation, qb, rb):
    blk = ATT_BLOCK
    width = n_heads * HEAD_DIM
    at_start = pl.program_id(1) == 0
    lane = lax.broadcasted_iota(jnp.int32, (blk, 128), 1)
    ones = jnp.ones((2 * blk, HEAD_DIM), BF16)
    for u, rr in [(u, rr) for u in range(qb) for rr in range(rb)]:
        start = u * blk * dilation + pl.program_id(2) * rb + rr
        rows = pl.ds(start, blk, stride=dilation) if dilation > 1 else slice(u * blk, (u + 1) * blk)
        cur = slice(u * blk, (u + 1) * blk)
        before = slice((u - 1) * blk, u * blk)

        def attend(with_prev, u=u, rr=rr, rows=rows, cur=cur, before=before):
            lse = jnp.zeros((blk, 128), F32)
            for h in range(n_heads):
                sl = slice(rr * width + h * HEAD_DIM, rr * width + (h + 1) * HEAD_DIM)
                k, v, bias = kc_ref[cur, sl], vc_ref[cur, sl], bias_ref[1, h]
                if with_prev:
                    k = jnp.concatenate([kp_ref[:, sl] if u == 0 else kc_ref[before, sl], k], axis=0)
                    v = jnp.concatenate([vp_ref[:, sl] if u == 0 else vc_ref[before, sl], v], axis=0)
                else:
                    bias = bias[:, blk:]
                s = lax.dot_general(q_ref[cur, sl], k, (((1,), (1,)), ((), ())), preferred_element_type=F32) + bias
                m = jnp.max(s, axis=-1, keepdims=True)
                p = jnp.exp2(s - m).astype(BF16)
                pv = jnp.dot(p, jnp.concatenate([v, ones[:v.shape[0]]], axis=1), preferred_element_type=F32)
                den = pv[:, HEAD_DIM:]
                o_ref[h, rows, :] = (pv[:, :HEAD_DIM] * (1.0 / den)).astype(o_ref.dtype)
                lse = jnp.where(lane == h, m + jnp.log(den) * LOG2_E, lse)
            lse_ref[rows, :] = lse

        if u == 0:
            pl.when(at_start)(functools.partial(attend, False))
            pl.when(jnp.logical_not(at_start))(functools.partial(attend, True))
        else:
            attend(True)


def _dilated_bias(window, dilation, n_heads):
    blk = ATT_BLOCK
    win_sub = window // dilation
    qi = jnp.arange(blk)[:, None]
    kj = jnp.arange(2 * blk)[None, :]
    dist = qi + blk - kj
    band = (dist >= 0) & (dist <= win_sub)
    slopes = jnp.exp2(-8.0 * (jnp.arange(n_heads, dtype=F32) + 1.0) / n_heads)
    bias = -slopes[:, None, None] * (dist * dilation).astype(F32)[None] * LOG2_E
    later = jnp.where(band[None], bias, NEG_BIG)
    first = jnp.where((band & (kj >= blk))[None], bias, NEG_BIG)
    return jnp.stack([first, later], axis=0)


def _dilated_branch(zd, t, window, dilation, *, n_heads, qb, rb):
    bsz = zd.shape[0]
    width = n_heads * HEAD_DIM
    d = dilation
    steps = t // (d * ATT_BLOCK * qb)
    span = qb * d * ATT_BLOCK
    groups = d // rb
    bias = _dilated_bias(window, d, n_heads)

    def cur(part):
        return pl.BlockSpec((None, qb * ATT_BLOCK, rb * width), lambda b, n, r: (b, n, part * groups + r))

    def prev(part):
        return pl.BlockSpec((None, ATT_BLOCK, rb * width),
                            lambda b, n, r: (b, jnp.maximum(qb * n - 1, 0), part * groups + r))

    return pl.pallas_call(
        functools.partial(_dilated_kernel, n_heads=n_heads, dilation=d, qb=qb, rb=rb),
        out_shape=(jax.ShapeDtypeStruct((bsz, n_heads, t, HEAD_DIM), BF16 if d == 1 else F32),
                   jax.ShapeDtypeStruct((bsz, t, 128), F32)),
        grid=(bsz, steps, groups),
        in_specs=[cur(0), prev(1), cur(1), prev(2), cur(2),
                  pl.BlockSpec((2, n_heads, ATT_BLOCK, 2 * ATT_BLOCK), lambda b, n, r: (0, 0, 0, 0))],
        out_specs=(pl.BlockSpec((None, n_heads, span, HEAD_DIM), lambda b, n, r: (b, 0, n, 0)),
                   pl.BlockSpec((None, span, 128), lambda b, n, r: (b, n, 0))),
        compiler_params=_params("parallel", "arbitrary", "arbitrary"),
        name=f"dilated_d{d}",
    )(zd, zd, zd, zd, zd, bias)


def _out_merge_kernel(a_ref, o1_ref, o2_ref, o3_ref, l1_ref, l2_ref, l3_ref, w_ref, x_ref, g_ref, out_ref,
                      *, n_heads, a_width, rc):
    for c in range(x_ref.shape[0] // rc):
        rows = slice(c * rc, (c + 1) * rc)
        l1, l2, l3 = l1_ref[rows, :], l2_ref[rows, :], l3_ref[rows, :]
        m = jnp.maximum(jnp.maximum(l1, l2), l3)
        e1, e2, e3 = jnp.exp2(l1 - m), jnp.exp2(l2 - m), jnp.exp2(l3 - m)
        inv = 1.0 / (e1 + e2 + e3)
        w1, w2, w3 = e1 * inv, e2 * inv, e3 * inv
        parts = []
        for h in range(n_heads):
            bh = (w1[:, h:h + 1] * o1_ref[h, rows, :] + w2[:, h:h + 1] * o2_ref[h, rows, :]
                  + w3[:, h:h + 1] * o3_ref[h, rows, :])
            parts.append(bh.astype(BF16))
        b_out = jnp.concatenate(parts, axis=1)
        y = (jnp.dot(a_ref[rows, :], w_ref[:a_width, :], preferred_element_type=F32)
             + jnp.dot(b_out, w_ref[a_width:, :], preferred_element_type=F32))
        out_ref[rows, :] = x_ref[rows, :] + y * _rms_scale(y) * g_ref[...]


def _out_merge(a_out, outs, lses, w_out, x, g, *, tm=512, rc=256):
    m, d = x.shape
    a_width = a_out.shape[1]
    bsz, n_heads, t, _ = outs[0].shape
    tiles = t // tm
    row = lambda width: pl.BlockSpec((tm, width), lambda i: (i, 0))
    heads = pl.BlockSpec((None, n_heads, tm, HEAD_DIM), lambda i: (i // tiles, 0, i % tiles, 0))
    return pl.pallas_call(
        functools.partial(_out_merge_kernel, n_heads=n_heads, a_width=a_width, rc=rc),
        out_shape=jax.ShapeDtypeStruct((m, d), F32),
        grid=(m // tm,),
        in_specs=[row(a_width), heads, heads, heads, row(128), row(128), row(128),
                  _const_spec(w_out.shape), row(d), _const_spec((1, d))],
        out_specs=row(d),
        compiler_params=_params("parallel"),
        name="out_merge",
    )(a_out, *outs, *lses, w_out, x, g)


def _out_proj_kernel(a_ref, w_ref, x_ref, g_ref, out_ref, *, rc):
    for c in range(x_ref.shape[0] // rc):
        rows = slice(c * rc, (c + 1) * rc)
        y = jnp.dot(a_ref[rows, :], w_ref[...], preferred_element_type=F32)
        out_ref[rows, :] = x_ref[rows, :] + y * _rms_scale(y) * g_ref[...]


def _out_proj(a, w_out, x, g, *, tm=512, rc=256):
    m, d = x.shape
    row = lambda width: pl.BlockSpec((tm, width), lambda i: (i, 0))
    return pl.pallas_call(
        functools.partial(_out_proj_kernel, rc=rc),
        out_shape=jax.ShapeDtypeStruct((m, d), F32),
        grid=(m // tm,),
        in_specs=[row(a.shape[1]), _const_spec(w_out.shape), row(d), _const_spec((1, d))],
        out_specs=row(d),
        compiler_params=_params("parallel"),
        name="out_proj",
    )(a, w_out, x, g)


def _ffn_kernel(x_ref, g1_ref, w1_ref, w2_ref, g2_ref, o_ref, h_ref, acc_ref, *, rc):
    f = pl.program_id(1)
    last = pl.num_programs(1) - 1
    chunks = [slice(c * rc, (c + 1) * rc) for c in range(x_ref.shape[0] // rc)]

    def mlp(h):
        t = jnp.maximum(jnp.dot(h, w1_ref[...], preferred_element_type=F32), 0.0)
        return jnp.dot((t * t).astype(BF16), w2_ref[...], preferred_element_type=F32)

    @pl.when(f == 0)
    def _():
        for rows in chunks:
            x = x_ref[rows, :]
            h = (x * _rms_scale(x) * g1_ref[...]).astype(h_ref.dtype)
            h_ref[rows, :] = h
            acc_ref[rows, :] = mlp(h)

    @pl.when(jnp.logical_and(f > 0, f < last))
    def _():
        acc_ref[...] += mlp(h_ref[...])

    @pl.when(f == last)
    def _():
        for rows in chunks:
            y = acc_ref[rows, :] + mlp(h_ref[rows, :])
            o_ref[rows, :] = x_ref[rows, :] + y * _rms_scale(y) * g2_ref[...]


def _ffn(x, g1, w1, w2, g2, *, tm=512, rc=256, tf=1024):
    m, d = x.shape
    d_ff = w1.shape[1]
    assert d_ff // tf >= 2, "the first and the last hidden slice are distinct grid steps"
    return pl.pallas_call(
        functools.partial(_ffn_kernel, rc=rc),
        out_shape=jax.ShapeDtypeStruct((m, d), F32),
        grid=(m // tm, d_ff // tf),
        in_specs=[
            pl.BlockSpec((tm, d), lambda i, f: (i, 0)),
            pl.BlockSpec((1, d), lambda i, f: (0, 0)),
            pl.BlockSpec((d, tf), lambda i, f: (0, f)),
            pl.BlockSpec((tf, d), lambda i, f: (f, 0)),
            pl.BlockSpec((1, d), lambda i, f: (0, 0)),
        ],
        out_specs=pl.BlockSpec((tm, d), lambda i, f: (i, 0)),
        scratch_shapes=[pltpu.VMEM((tm, d), BF16), pltpu.VMEM((tm, d), F32)],
        compiler_params=_params("parallel", "arbitrary"),
        name="ffn",
    )(x, g1, w1, w2, g2)


SB_ZERO_SUFFIX_BITS = 152.0


def _sb_kernel(*refs, blk, heads, cast_blocks):
    n_cast = len(cast_blocks)
    q_ref, k_ref, v_ref, tri_ref = refs[:4]
    o_ref = refs[4 + n_cast]
    step = (pl.program_id(0) * pl.num_programs(1) + pl.program_id(1)) * pl.num_programs(2) + pl.program_id(2)
    _side_cast(step, refs[4:4 + n_cast], refs[5 + n_cast:], cast_blocks)
    i = pl.program_id(2)
    tri = tri_ref[...]
    row = lax.broadcasted_iota(jnp.int32, (blk, blk), 0)
    col = lax.broadcasted_iota(jnp.int32, (blk, blk), 1)
    causal = col < row

    def blocks(jobs):
        cols = [slice(h * HEAD_DIM, (h + 1) * HEAD_DIM) for h, _, _, _ in jobs]
        starts = [pl.multiple_of(j * blk, blk) for _, j, _, _ in jobs]
        z2s = [lax.dot_general(q_ref[:, sl], k_ref[pl.ds(st, blk), sl], (((1,), (1,)), ((), ())),
                               preferred_element_type=F32) for sl, st in zip(cols, starts)]
        sps = [jnp.maximum(z2, 0.0) + jnp.log(1.0 + jnp.exp2(-jnp.abs(z2))) * LOG2_E for z2 in z2s]
        sps = [jnp.where(causal, sp, 0.0) if job[2] else sp for sp, job in zip(sps, jobs)]
        sums = [jnp.dot(sp.astype(BF16), tri, preferred_element_type=F32) for sp in sps]
        cums = []
        for part, (_, _, _, suffix) in zip(sums, jobs):
            if suffix is None:
                cums.append(part)
            else:
                cums.append(part + (cums[suffix][:, 0:1] if isinstance(suffix, int) else suffix))
        weights = [jnp.exp2(z2 - cum) for z2, cum in zip(z2s, cums)]
        weights = [jnp.where(causal, a, 0.0) if job[2] else a for a, job in zip(weights, jobs)]
        outs = [jnp.dot(a.astype(BF16), v_ref[pl.ds(st, blk), sl], preferred_element_type=F32)
                for a, sl, st in zip(weights, cols, starts)]
        return [(o, cum[:, 0:1]) for o, cum in zip(outs, cums)]

    def store(accs):
        for h in range(heads):
            o_ref[:, h * HEAD_DIM:(h + 1) * HEAD_DIM] = accs[h].astype(o_ref.dtype)

    @pl.when(i == 0)
    def _():
        store([o for o, _ in blocks([(h, 0, True, None) for h in range(heads)])])

    @pl.when(i > 0)
    def _():
        res = blocks([job for h in range(heads) for job in ((h, i, True, None), (h, i - 1, False, 2 * h))])
        accs = tuple(res[2 * h][0] + res[2 * h + 1][0] for h in range(heads))
        sufs = tuple(res[2 * h + 1][1] for h in range(heads))

        def cond(carry):
            j, _, sufs = carry
            smallest = functools.reduce(jnp.minimum, sufs)
            return jnp.logical_and(j >= 0, jnp.min(smallest) < SB_ZERO_SUFFIX_BITS)

        def body(carry):
            j, accs, sufs = carry
            new = blocks([(h, j, False, sufs[h]) for h in range(heads)])
            return j - 1, tuple(accs[h] + new[h][0] for h in range(heads)), tuple(n[1] for n in new)

        _, accs, _ = lax.while_loop(cond, body, (i - 2, accs, sufs))
        store(accs)


def _stick_breaking(qkv, bsz, t, *, n_heads, cast=(), blk=256, heads=4):
    nq = t // blk
    groups = n_heads // heads
    width = heads * HEAD_DIM
    ar = jnp.arange(blk)
    tri = (ar[:, None] >= ar[None, :]).astype(BF16)
    cast_in, cast_out, cast_shapes, cast_blocks = _side_cast_plan(
        cast, bsz * groups * nq, lambda b, g, i: (b * groups + g) * nq + i)
    return pl.pallas_call(
        functools.partial(_sb_kernel, blk=blk, heads=heads, cast_blocks=tuple(cast_blocks)),
        out_shape=(jax.ShapeDtypeStruct((bsz * t, n_heads * HEAD_DIM), BF16), *cast_shapes),
        grid=(bsz, groups, nq),
        in_specs=[
            pl.BlockSpec((blk, width), lambda b, g, i: (b * nq + i, g)),
            pl.BlockSpec((t, width), lambda b, g, i: (b, groups + g)),
            pl.BlockSpec((t, width), lambda b, g, i: (b, 2 * groups + g)),
            pl.BlockSpec((blk, blk), lambda b, g, i: (0, 0)),
            *cast_in,
        ],
        out_specs=(pl.BlockSpec((blk, width), lambda b, g, i: (b * nq + i, g)), *cast_out),
        compiler_params=_params("arbitrary", "arbitrary", "arbitrary"),
        name="stick_breaking",
    )(qkv, qkv, qkv, tri, *[w for w, _ in cast])


def kernel(x, norm_pre_mix, norm_post_mix, norm_pre_ffn, norm_post_ffn, ab_w_in, sgu_ln_g, sgu_ln_b, sgu_w, sgu_b,
           ab_w_out, sb_w_in, sb_w_out, ffn_w1, ffn_w2):
    bsz, t, d = x.shape
    a_width = sgu_ln_g.shape[-1]
    b_heads = (ab_w_in.shape[-1] - 2 * a_width) // (3 * HEAD_DIM)
    c_heads = sb_w_in.shape[-1] // (3 * HEAD_DIM)
    assert a_width == b_heads * HEAD_DIM, "dilated q/k/v column blocks are addressed in units of a_width"

    xs = x.reshape(bsz * t, d)
    row = lambda v: v.reshape(1, -1).astype(F32)

    assert tuple(dil for _, dil in DILATED_PAIRS) == (1, 4, 16), "proj_qkv0 regroups q/k/v for dilations 1, 4, 16"
    w_in = _to_bf16(ab_w_in, 0)
    uv, w1_0, w_in1 = _proj_uv(xs, row(norm_pre_mix[0]), w_in, row(sgu_ln_g[0]), row(sgu_ln_b[0]),
                               cast=[(ffn_w1, 0), (sb_w_in, 0)])
    *grouped, w2_0, w_out0 = _proj_qkv0(xs, row(norm_pre_mix[0]), w_in, bsz, t, tn=a_width, first_block=2,
                                        q_scale=HEAD_DIM ** -0.5 * LOG2_E, cast=[(ffn_w2, 0), (ab_w_out, 0)])
    bias_full = jnp.repeat(sgu_b[0].T.astype(F32), CHUNK, axis=1)
    a_out = _sgu(uv, sgu_w[0], bias_full, a_width=a_width)
    outs, lses = zip(*[_dilated_branch(zd, t, window, dilation, n_heads=b_heads, qb=qb, rb=rb)
                       for zd, (window, dilation), (qb, rb) in zip(grouped, DILATED_PAIRS, DILATED_STEP)])
    lses = [lse.reshape(bsz * t, 128) for lse in lses]
    xs = _out_merge(a_out, outs, lses, w_out0, xs, row(norm_post_mix[0]))
    xs = _ffn(xs, row(norm_pre_ffn[0]), w1_0, w2_0, row(norm_post_ffn[0]))

    qkv = _norm_matmul(xs, row(norm_pre_mix[1]), w_in1, q_cols=c_heads * HEAD_DIM,
                       q_scale=HEAD_DIM ** -0.5 * LOG2_E)
    att, w_out1, w1_1, w2_1 = _stick_breaking(qkv, bsz, t, n_heads=c_heads,
                                              cast=[(sb_w_out, 0), (ffn_w1, 1), (ffn_w2, 1)])
    xs = _out_proj(att, w_out1, xs, row(norm_post_mix[1]))
    xs = _ffn(xs, row(norm_pre_ffn[1]), w1_1, w2_1, row(norm_post_ffn[1]))
    return xs.reshape(bsz, t, d)
```
